```python
import math
import jax, jax.numpy as jnp
from jax import lax
import numpy as np

D_MODEL = 1024
BATCH = 16
SEQ = 2048
DEPTH = 4

N_EVEN = (DEPTH + 1) // 2
N_ODD = DEPTH // 2

CONV_WIDTH = D_MODEL
LRU_WIDTH = D_MODEL
CONV_KERNEL = 31
LRU_CONV_KERNEL = 4
LRU_HEADS = 16
LRU_HEAD_DIM = LRU_WIDTH // LRU_HEADS
LRU_C = 8.0
EVEN_IN = 3 * CONV_WIDTH + 2 * LRU_WIDTH
EVEN_OUT = CONV_WIDTH + LRU_WIDTH

MLA_HEADS = 16
QK_NOPE = 64
QK_ROPE = 32
V_HEAD = 64
Q_LORA = 256
KV_LORA = 256
MLA_WIDTH = MLA_HEADS * V_HEAD
ODD_IN = Q_LORA + KV_LORA + QK_ROPE + MLA_WIDTH
ROPE_THETA = 10000.0
Q_BLOCK = 128
EPS = 1e-6

kernel_name = 'hybrid_conv_rglru_mla_adaln_trunk'

F32 = jnp.float32


def rms_norm(x, g):
    xf = x.astype(F32)
    y = xf * lax.rsqrt(jnp.mean(xf * xf, axis=-1, keepdims=True) + EPS)
    return (y * g.astype(F32)).astype(x.dtype)


def layer_norm(x, g, b):
    xf = x.astype(F32)
    mu = jnp.mean(xf, axis=-1, keepdims=True)
    xc = xf - mu
    y = xc * lax.rsqrt(jnp.mean(xc * xc, axis=-1, keepdims=True) + EPS)
    return (y * g.astype(F32) + b.astype(F32)).astype(x.dtype)


def causal_depthwise_conv(x, w, b):
    k = w.shape[0]
    y = lax.conv_general_dilated(
        x, w[:, None, :], window_strides=(1,), padding=[(k - 1, 0)],
        dimension_numbers=('NWC', 'WIO', 'NWC'), feature_group_count=x.shape[-1])
    return y + b


def _linear_recurrence_combine(left, right):
    a_l, b_l = left
    a_r, b_r = right
    return a_l * a_r, a_r * b_l + b_r


def rg_lru(x, wa, ba, wx, bx, lam):
    bsz, s, e = x.shape
    xh = x.reshape(bsz, s, LRU_HEADS, LRU_HEAD_DIM)
    r = jax.nn.sigmoid((jnp.einsum('bshi,hij->bshj', xh, wa).reshape(bsz, s, e) + ba).astype(F32))
    i = jax.nn.sigmoid((jnp.einsum('bshi,hij->bshj', xh, wx).reshape(bsz, s, e) + bx).astype(F32))
    log_a = -LRU_C * r * jax.nn.softplus(-lam.astype(F32))
    a = jnp.exp(log_a)
    mult = jnp.sqrt(-jnp.expm1(2.0 * log_a))
    b = mult * i * x.astype(F32)
    _, h = lax.associative_scan(_linear_recurrence_combine, (a, b), axis=1)
    return h.astype(x.dtype)


def conv_lru_mixer(h, w_in, conv_w, conv_b, ln_g, ln_b, lru_conv_w, lru_conv_b,
                   lru_wa, lru_ba, lru_wx, lru_bx, lru_lam, w_out):
    u = h @ w_in
    c0, c1, c2, c3 = CONV_WIDTH, 2 * CONV_WIDTH, 3 * CONV_WIDTH, 3 * CONV_WIDTH + LRU_WIDTH
    va, ga, za, xb, zb = jnp.split(u, [c0, c1, c2, c3], axis=-1)
    a = va * jax.nn.sigmoid(ga)
    a = causal_depthwise_conv(a, conv_w, conv_b)
    a = layer_norm(a, ln_g, ln_b)
    a = jax.nn.silu(a) * jax.nn.silu(za)
    xb = causal_depthwise_conv(xb, lru_conv_w, lru_conv_b)
    yb = rg_lru(xb, lru_wa, lru_ba, lru_wx, lru_bx, lru_lam) * jax.nn.silu(zb)
    return jnp.concatenate([a, yb], axis=-1) @ w_out


def rope_cos_sin(positions):
    inv = ROPE_THETA ** (-jnp.arange(0, QK_ROPE, 2, dtype=F32) / QK_ROPE)
    ang = positions.astype(F32)[..., None] * inv
    return jnp.cos(ang), jnp.sin(ang)


def apply_rope(x, cos, sin):
    xf = x.astype(F32)
    x1, x2 = jnp.split(xf, 2, axis=-1)
    return jnp.concatenate([x1 * cos - x2 * sin, x1 * sin + x2 * cos], axis=-1).astype(x.dtype)


def causal_mla_attention(q_nope, q_rope, k_nope, k_rope, v):
    s_len = q_nope.shape[1]
    scale = (QK_NOPE + QK_ROPE) ** -0.5
    outs = []
    for blk in range(s_len // Q_BLOCK):
        q0 = blk * Q_BLOCK
        kend = q0 + Q_BLOCK
        s = (jnp.einsum('bqhd,bkhd->bhqk', q_nope[:, q0:kend], k_nope[:, :kend])
             + jnp.einsum('bqhr,bkr->bhqk', q_rope[:, q0:kend], k_rope[:, :kend]))
        s = s.astype(F32) * scale
        mask = jnp.arange(kend)[None, :] <= (q0 + jnp.arange(Q_BLOCK))[:, None]
        s = jnp.where(mask, s, -jnp.inf)
        p = jax.nn.softmax(s, axis=-1).astype(v.dtype)
        outs.append(jnp.einsum('bhqk,bkhd->bqhd', p, v[:, :kend]))
    return jnp.concatenate(outs, axis=1)


def mla_mixer(h, cos, sin, w_in, q_norm, kv_norm, w_uq, w_ukv, w_out):
    bsz, s, _ = h.shape
    u = h @ w_in
    cq, ckv, k_rope, z = jnp.split(u, [Q_LORA, Q_LORA + KV_LORA, Q_LORA + KV_LORA + QK_ROPE], axis=-1)
    q = (rms_norm(cq, q_norm) @ w_uq).reshape(bsz, s, MLA_HEADS, QK_NOPE + QK_ROPE)
    kv = (rms_norm(ckv, kv_norm) @ w_ukv).reshape(bsz, s, MLA_HEADS, QK_NOPE + V_HEAD)
    q_nope, q_rope = jnp.split(q, [QK_NOPE], axis=-1)
    k_nope, v = jnp.split(kv, [QK_NOPE], axis=-1)
    q_rope = apply_rope(q_rope, cos[:, :, None, :], sin[:, :, None, :])
    k_rope = apply_rope(k_rope, cos, sin)
    o = causal_mla_attention(q_nope, q_rope, k_nope, k_rope, v).reshape(bsz, s, MLA_WIDTH)
    return (o * jax.nn.silu(z)) @ w_out


def setup_inputs(seed: int = 0) -> dict:
    key = jax.random.key(seed)
    ks = iter(jax.random.split(key, 40))

    def nrm(shape, scale):
        return jax.random.normal(next(ks), shape, F32) * scale

    def gain(shape):
        return 1.0 + 0.05 * jax.random.normal(next(ks), shape, F32)

    x = nrm((BATCH, SEQ, D_MODEL), 1.0)
    c = nrm((BATCH, D_MODEL), 1.0)
    offsets = jax.random.randint(next(ks), (BATCH, 1), 0, 4096, dtype=jnp.int32)
    positions = (offsets + jnp.arange(SEQ, dtype=jnp.int32)[None, :]).astype(jnp.int32)

    u = jax.random.uniform(next(ks), (N_EVEN, LRU_WIDTH), F32, minval=0.9, maxval=0.999)
    a_base = u ** (1.0 / LRU_C)
    lru_lam = jnp.log(a_base) - jnp.log1p(-a_base)

    return {
        'x': x,
        'c': c,
        'positions': positions,
        'ada_w': nrm((DEPTH, D_MODEL, 3 * D_MODEL), 0.5 * D_MODEL ** -0.5),
        'ada_b': nrm((DEPTH, 3 * D_MODEL), 0.02),
        'pre_g': gain((DEPTH, D_MODEL)),
        'post_g': gain((DEPTH, D_MODEL)),
        'ev_w_in': nrm((N_EVEN, D_MODEL, EVEN_IN), D_MODEL ** -0.5),
        'ev_conv_w': nrm((N_EVEN, CONV_KERNEL, CONV_WIDTH), CONV_KERNEL ** -0.5),
        'ev_conv_b': nrm((N_EVEN, CONV_WIDTH), 0.02),
        'ev_ln_g': gain((N_EVEN, CONV_WIDTH)),
        'ev_ln_b': nrm((N_EVEN, CONV_WIDTH), 0.02),
        'ev_lru_conv_w': nrm((N_EVEN, LRU_CONV_KERNEL, LRU_WIDTH), LRU_CONV_KERNEL ** -0.5),
        'ev_lru_conv_b': nrm((N_EVEN, LRU_WIDTH), 0.02),
        'ev_lru_wa': nrm((N_EVEN, LRU_HEADS, LRU_HEAD_DIM, LRU_HEAD_DIM), LRU_HEAD_DIM ** -0.5),
        'ev_lru_ba': nrm((N_EVEN, LRU_WIDTH), 0.02),
        'ev_lru_wx': nrm((N_EVEN, LRU_HEADS, LRU_HEAD_DIM, LRU_HEAD_DIM), LRU_HEAD_DIM ** -0.5),
        'ev_lru_bx': nrm((N_EVEN, LRU_WIDTH), 0.02),
        'ev_lru_lam': lru_lam,
        'ev_w_out': nrm((N_EVEN, EVEN_OUT, D_MODEL), EVEN_OUT ** -0.5),
        'od_w_in': nrm((N_ODD, D_MODEL, ODD_IN), D_MODEL ** -0.5),
        'od_q_norm': gain((N_ODD, Q_LORA)),
        'od_kv_norm': gain((N_ODD, KV_LORA)),
        'od_w_uq': nrm((N_ODD, Q_LORA, MLA_HEADS * (QK_NOPE + QK_ROPE)), Q_LORA ** -0.5),
        'od_w_ukv': nrm((N_ODD, KV_LORA, MLA_HEADS * (QK_NOPE + V_HEAD)), KV_LORA ** -0.5),
        'od_w_out': nrm((N_ODD, MLA_WIDTH, D_MODEL), MLA_WIDTH ** -0.5),
    }


def reference(x, c, positions, ada_w, ada_b, pre_g, post_g,
              ev_w_in, ev_conv_w, ev_conv_b, ev_ln_g, ev_ln_b, ev_lru_conv_w, ev_lru_conv_b,
              ev_lru_wa, ev_lru_ba, ev_lru_wx, ev_lru_bx, ev_lru_lam, ev_w_out,
              od_w_in, od_q_norm, od_kv_norm, od_w_uq, od_w_ukv, od_w_out):
    cos, sin = rope_cos_sin(positions)
    c_act = jax.nn.silu(c)
    for layer in range(DEPTH):
        mod = c_act @ ada_w[layer] + ada_b[layer]
        shift, scale, gate = jnp.split(mod, 3, axis=-1)
        h = rms_norm(x, pre_g[layer]) * (1.0 + scale[:, None, :]) + shift[:, None, :]
        j = layer // 2
        if layer % 2 == 0:
            y = conv_lru_mixer(h, ev_w_in[j], ev_conv_w[j], ev_conv_b[j], ev_ln_g[j], ev_ln_b[j],
                               ev_lru_conv_w[j], ev_lru_conv_b[j], ev_lru_wa[j], ev_lru_ba[j],
                               ev_lru_wx[j], ev_lru_bx[j], ev_lru_lam[j], ev_w_out[j])
        else:
            y = mla_mixer(h, cos, sin, od_w_in[j], od_q_norm[j], od_kv_norm[j],
                          od_w_uq[j], od_w_ukv[j], od_w_out[j])
        x = x + gate[:, None, :] * rms_norm(y, post_g[layer])
    return x
```

```python
import functools

import jax
import jax.numpy as jnp
from jax import lax
from jax.experimental import pallas as pl
from jax.experimental.pallas import tpu as pltpu

F32 = jnp.float32
BF16 = jnp.bfloat16

D_MODEL = 1024
CONV_KERNEL = 31
LRU_CONV_KERNEL = 4
LRU_HEADS = 16
LRU_HEAD_DIM = D_MODEL // LRU_HEADS
LRU_C = 8.0
MLA_HEADS = 16
QK_NOPE = 64
QK_ROPE = 32
V_HEAD = 64
Q_LORA = 256
KV_LORA = 256
ROPE_THETA = 10000.0
EPS = 1e-6

LANE = 128
SUBLANE = 8
MXU_DIM = 256

SEQ_TILE = 256
N_SLAB = D_MODEL // LANE
CONV_HALO = 32
LRU_HALO = 8
CONV_TAPS_PAD = 32
GATE_GROUP = MXU_DIM
N_GATE_GROUP = D_MODEL // GATE_GROUP
QK_PAD = LANE
HEAD_PAIR = 2
NEG_BIG = -1e30

VMEM_LIMIT_BYTES = 56 * 1024 * 1024


def _dot(a, b):
    return jnp.dot(a, b, preferred_element_type=F32)


def _dot_nt(a, b):
    return lax.dot_general(a, b, (((1,), (1,)), ((), ())), preferred_element_type=F32)


def _sigmoid(x):
    return 0.5 * jnp.tanh(0.5 * x) + 0.5


def _silu(x):
    return x * _sigmoid(x)


def _const_spec(shape):
    nd = len(shape)
    return pl.BlockSpec(shape, lambda *_: (0,) * nd, pipeline_mode=pl.Buffered(1))


def _params(n_grid):
    return pltpu.CompilerParams(
        dimension_semantics=("arbitrary",) * n_grid, vmem_limit_bytes=VMEM_LIMIT_BYTES)


def _ada_kernel(c_ref, w_ref, b_ref, o_ref):
    c = c_ref[...]
    o_ref[0] = jnp.dot(_silu(c), w_ref[0], preferred_element_type=F32,
                       precision=lax.Precision.HIGHEST) + b_ref[0]


def _ada_call(c, ada_w, ada_b):
    depth, d, d3 = ada_w.shape
    bsz = c.shape[0]
    n_col = d3 // d
    return pl.pallas_call(
        _ada_kernel,
        grid=(depth, n_col),
        in_specs=[
            pl.BlockSpec((bsz, d), lambda l, j: (0, 0)),
            pl.BlockSpec((1, d, d), lambda l, j: (l, 0, j)),
            pl.BlockSpec((1, 1, d), lambda l, j: (l, 0, j)),
        ],
        out_specs=pl.BlockSpec((1, bsz, d), lambda l, j: (l, 0, j)),
        out_shape=jax.ShapeDtypeStruct((depth, bsz, d3), F32),
        compiler_params=_params(2),
        name="ada_mod",
    )(c, ada_w, ada_b.reshape(depth, 1, d3))


def _rope_kernel(posrep_ref, invrep_ref, pos_ref, invcol_ref, cn_ref, sn_ref, ct_ref, st_ref):
    ang_n = posrep_ref[0].astype(F32) * invrep_ref[...]
    cn_ref[0] = jnp.cos(ang_n)
    sn_ref[0] = jnp.sin(ang_n)
    ang_t = invcol_ref[...] * pos_ref[0].astype(F32)
    ct_ref[0] = jnp.cos(ang_t)
    st_ref[0] = jnp.sin(ang_t)


def _rope_call(positions):
    bsz, seq = positions.shape
    half = QK_ROPE // 2
    rep = LANE // half
    inv = ROPE_THETA ** (-jnp.arange(0, QK_ROPE, 2, dtype=F32) / QK_ROPE)
    posrep = jnp.repeat(positions, half, axis=1).reshape(bsz, seq // rep, LANE)
    invrep = jnp.tile(inv, rep).reshape(1, LANE)
    n_spec = pl.BlockSpec((1, seq // rep, LANE), lambda b: (b, 0, 0))
    t_spec = pl.BlockSpec((1, half, seq), lambda b: (b, 0, 0))
    cn, sn, ct, st = pl.pallas_call(
        _rope_kernel,
        grid=(bsz,),
        in_specs=[
            n_spec,
            pl.BlockSpec((1, LANE), lambda b: (0, 0)),
            pl.BlockSpec((1, 1, seq), lambda b: (b, 0, 0)),
            pl.BlockSpec((half, 1), lambda b: (0, 0)),
        ],
        out_specs=[n_spec, n_spec, t_spec, t_spec],
        out_shape=[jax.ShapeDtypeStruct((bsz, seq // rep, LANE), F32)] * 2
        + [jax.ShapeDtypeStruct((bsz, half, seq), F32)] * 2,
        compiler_params=_params(1),
        name="rope_tables",
    )(posrep, invrep, positions.reshape(bsz, 1, seq), inv.reshape(half, 1))
    return cn.reshape(bsz, seq, half), sn.reshape(bsz, seq, half), ct, st


def _prenorm(x, mod_ref, pre_g_ref):
    ms = jnp.mean(x * x, axis=-1, keepdims=True)
    shift = mod_ref[0, 0:1, :]
    scale = mod_ref[0, 1:2, :]
    gmul = pre_g_ref[...] * (1.0 + scale)
    return x * lax.rsqrt(ms + EPS) * gmul + shift


def _postnorm_residual(x, y, mod_ref, post_g_ref):
    ms = jnp.mean(y * y, axis=-1, keepdims=True)
    gate = mod_ref[0, 2:3, :]
    return x + (gate * post_g_ref[...]) * (y * lax.rsqrt(ms + EPS))


def _scan_slab(a, b, carry):
    t = a.shape[0]
    row = lax.broadcasted_iota(jnp.int32, a.shape, 0) & (SUBLANE - 1)
    d = 1
    while d < SUBLANE:
        a_sh = pltpu.roll(a, d, axis=0)
        b_sh = pltpu.roll(b, d, axis=0)
        ok = row >= d
        b = jnp.where(ok, a * b_sh + b, b)
        a = jnp.where(ok, a * a_sh, a)
        d *= 2
    outs = []
    c = carry
    for v in range(t // SUBLANE):
        lo = v * SUBLANE
        hv = b[lo:lo + SUBLANE] + a[lo:lo + SUBLANE] * c
        outs.append(hv)
        c = jnp.broadcast_to(hv[SUBLANE - 1:SUBLANE, :], (SUBLANE, LANE))
    return jnp.concatenate(outs, axis=0), c


def _even_kernel(x_ref, mod_ref, pre_g_ref, post_g_ref, w_in_ref, cw_ref, cb_ref, lng_ref, lnb_ref,
                 lcw_ref, lcb_ref, wg_ref, ba_ref, bx_ref, lam_ref, w_out_ref, o_ref,
                 hbf, u0, u1, u2, cbuf, cv, lbuf, xcf, xcb, a_s, b_s, hcar, act):
    ts = x_ref.shape[1]
    d = D_MODEL

    @pl.when(pl.program_id(1) == 0)
    def _():
        cbuf[:, 0:CONV_HALO, :] = jnp.zeros((N_SLAB, CONV_HALO, LANE), F32)
        lbuf[:, 0:LRU_HALO, :] = jnp.zeros((N_SLAB, LRU_HALO, LANE), F32)
        hcar[...] = jnp.zeros(hcar.shape, F32)

    x = x_ref[0]
    hbf[...] = _prenorm(x, mod_ref, pre_g_ref).astype(BF16)

    u0[...] = _dot(hbf[...], w_in_ref[:, 0:d])
    u1[...] = _dot(hbf[...], w_in_ref[:, d:2 * d])
    for c in range(N_SLAB):
        sl = slice(c * LANE, (c + 1) * LANE)
        cbuf[c, CONV_HALO:CONV_HALO + ts, :] = u0[:, sl] * _sigmoid(u1[:, sl])
    u2[...] = _dot(hbf[...], w_in_ref[:, 2 * d:3 * d])
    u0[...] = _dot(hbf[...], w_in_ref[:, 3 * d:4 * d])
    for c in range(N_SLAB):
        lbuf[c, LRU_HALO:LRU_HALO + ts, :] = u0[:, c * LANE:(c + 1) * LANE]
    u1[...] = _dot(hbf[...], w_in_ref[:, 4 * d:5 * d])

    half_rows = ts // 2

    def conv_body(c, carry):
        w = cw_ref[c]
        for r0 in (0, half_rows):
            acc = jnp.broadcast_to(cb_ref[c], (half_rows, LANE))
            for j in range(CONV_KERNEL):
                start = CONV_HALO - (CONV_KERNEL - 1) + j + r0
                acc = acc + cbuf[c, start:start + half_rows, :] * w[j:j + 1, :]
            cv[c, r0:r0 + half_rows, :] = acc
        return carry

    lax.fori_loop(0, N_SLAB, conv_body, 0)

    tot = cv[0]
    for c in range(1, N_SLAB):
        tot = tot + cv[c]
    mu = jnp.sum(tot, axis=-1, keepdims=True) * (1.0 / d)
    sq = jnp.zeros((ts, LANE), F32)
    for c in range(N_SLAB):
        dc = cv[c] - mu
        sq = sq + dc * dc
    rstd = lax.rsqrt(jnp.sum(sq, axis=-1, keepdims=True) * (1.0 / d) + EPS)
    for c in range(N_SLAB):
        sl = slice(c * LANE, (c + 1) * LANE)
        yn = (cv[c] - mu) * rstd * lng_ref[:, sl] + lnb_ref[:, sl]
        act[:, sl] = (_silu(yn) * _silu(u2[:, sl])).astype(BF16)

    for c in range(N_SLAB):
        sl = slice(c * LANE, (c + 1) * LANE)
        acc = jnp.broadcast_to(lcb_ref[:, sl], (ts, LANE))
        for j in range(LRU_CONV_KERNEL):
            start = LRU_HALO - (LRU_CONV_KERNEL - 1) + j
            acc = acc + lbuf[c, start:start + ts, :] * lcw_ref[j:j + 1, sl]
        xcf[:, sl] = acc
        xcb[:, sl] = acc.astype(BF16)

    lam = lam_ref[...]
    nl = -lam
    softplus = jnp.maximum(nl, 0.0) + jnp.log1p(jnp.exp(-jnp.abs(nl)))
    for g in range(N_GATE_GROUP):
        sl = slice(g * GATE_GROUP, (g + 1) * GATE_GROUP)
        rg = _dot(xcb[:, sl], wg_ref[g])
        r = _sigmoid(rg[:, 0:GATE_GROUP] + ba_ref[:, sl])
        i = _sigmoid(rg[:, GATE_GROUP:2 * GATE_GROUP] + bx_ref[:, sl])
        a = jnp.exp((-LRU_C) * r * softplus[:, sl])
        a_s[:, sl] = a
        b_s[:, sl] = jnp.sqrt(1.0 - a * a) * i * xcf[:, sl]

    for c in range(N_SLAB):
        sl = slice(c * LANE, (c + 1) * LANE)
        hs, c_new = _scan_slab(a_s[:, sl], b_s[:, sl], hcar[:, sl])
        hcar[:, sl] = c_new
        act[:, d + c * LANE:d + (c + 1) * LANE] = (hs * _silu(u1[:, sl])).astype(BF16)

    cbuf[:, 0:CONV_HALO, :] = cbuf[:, ts:ts + CONV_HALO, :]
    lbuf[:, 0:LRU_HALO, :] = lbuf[:, ts:ts + LRU_HALO, :]

    y = _dot(act[...], w_out_ref[...])
    o_ref[0] = _postnorm_residual(x_ref[0], y, mod_ref, post_g_ref)


def _even_call(x, mod, pre_g, post_g, w_in, conv_w, conv_b, ln_g, ln_b, lcw, lcb,
               wa, ba, wx, bx, lam, w_out):
    bsz, seq, d = x.shape
    ts = SEQ_TILE
    row = lambda v: v.reshape(1, d)
    cw = jnp.pad(conv_w, ((0, CONV_TAPS_PAD - CONV_KERNEL), (0, 0)))
    cw = cw.reshape(CONV_TAPS_PAD, N_SLAB, LANE).transpose(1, 0, 2)
    cb = conv_b.reshape(N_SLAB, 1, LANE)
    hpg = GATE_GROUP // LRU_HEAD_DIM
    eye = jnp.eye(hpg, dtype=F32)

    def blockdiag(w):
        wg = w.reshape(N_GATE_GROUP, hpg, LRU_HEAD_DIM, LRU_HEAD_DIM)
        return jnp.einsum('ghij,hk->ghikj', wg, eye).reshape(N_GATE_GROUP, GATE_GROUP, GATE_GROUP)

    wg = jnp.concatenate([blockdiag(wa), blockdiag(wx)], axis=-1).astype(BF16)

    tile = pl.BlockSpec((1, ts, d), lambda b, s: (b, s, 0))
    scratch = [
        pltpu.VMEM((ts, d), BF16),
        pltpu.VMEM((ts, d), F32),
        pltpu.VMEM((ts, d), F32),
        pltpu.VMEM((ts, d), F32),
        pltpu.VMEM((N_SLAB, CONV_HALO + ts, LANE), F32),
        pltpu.VMEM((N_SLAB, ts, LANE), F32),
        pltpu.VMEM((N_SLAB, LRU_HALO + ts, LANE), F32),
        pltpu.VMEM((ts, d), F32),
        pltpu.VMEM((ts, d), BF16),
        pltpu.VMEM((ts, d), F32),
        pltpu.VMEM((ts, d), F32),
        pltpu.VMEM((SUBLANE, d), F32),
        pltpu.VMEM((ts, 2 * d), BF16),
    ]
    return pl.pallas_call(
        _even_kernel,
        grid=(bsz, seq // ts),
        in_specs=[
            tile,
            pl.BlockSpec((1, 3, d), lambda b, s: (b, 0, 0)),
            _const_spec((1, d)), _const_spec((1, d)),
            _const_spec(w_in.shape),
            _const_spec(cw.shape), _const_spec(cb.shape),
            _const_spec((1, d)), _const_spec((1, d)),
            _const_spec(lcw.shape), _const_spec((1, d)),
            _const_spec(wg.shape),
            _const_spec((1, d)), _const_spec((1, d)), _const_spec((1, d)),
            _const_spec(w_out.shape),
        ],
        out_specs=tile,
        out_shape=jax.ShapeDtypeStruct(x.shape, F32),
        scratch_shapes=scratch,
        compiler_params=_params(2),
        name="even_layer",
    )(x, mod, row(pre_g), row(post_g), w_in.astype(BF16), cw, cb, row(ln_g), row(ln_b),
      lcw, row(lcb), wg, row(ba), row(bx), row(lam), w_out.astype(BF16))


def _mla_proj_kernel(x_ref, mod_ref, pre_g_ref, w_c_ref, w_z_ref, qn_ref, kvn_ref,
                     wuq_t_ref, wk_ref, wv_t_ref, cn_ref, sn_ref, ct_ref, st_ref,
                     qt_ref, k_ref, vt_ref, sz_ref, hbf):
    ts = x_ref.shape[1]
    half = QK_ROPE // 2
    hbf[...] = _prenorm(x_ref[0], mod_ref, pre_g_ref).astype(BF16)

    uc = _dot(hbf[...], w_c_ref[...])
    sz_ref[0] = _silu(_dot(hbf[...], w_z_ref[...])).astype(BF16)

    def rms(v, g_ref):
        ms = jnp.mean(v * v, axis=-1, keepdims=True)
        return (v * lax.rsqrt(ms + EPS) * g_ref[...]).astype(BF16)

    cqn = rms(uc[:, 0:Q_LORA], qn_ref)
    ckvn = rms(uc[:, Q_LORA:Q_LORA + KV_LORA], kvn_ref)
    kr = uc[:, Q_LORA + KV_LORA:Q_LORA + KV_LORA + QK_ROPE]

    q_t = _dot_nt(wuq_t_ref[...], cqn)
    c_t = ct_ref[0]
    s_t = st_ref[0]
    scale = (QK_NOPE + QK_ROPE) ** -0.5
    for hd in range(MLA_HEADS):
        base = hd * QK_PAD
        x1 = q_t[base + QK_NOPE:base + QK_NOPE + half]
        x2 = q_t[base + QK_NOPE + half:base + QK_NOPE + QK_ROPE]
        blk = jnp.concatenate([
            q_t[base:base + QK_NOPE],
            x1 * c_t - x2 * s_t,
            x1 * s_t + x2 * c_t,
            q_t[base + QK_NOPE + QK_ROPE:base + QK_PAD],
        ], axis=0)
        qt_ref[0, hd, 0] = (blk * scale).astype(BF16)

    kv_k = _dot(ckvn, wk_ref[...])
    c_n = cn_ref[0]
    s_n = sn_ref[0]
    k1 = kr[:, 0:half]
    k2 = kr[:, half:QK_ROPE]
    kr128 = jnp.concatenate([
        jnp.zeros((ts, QK_NOPE), F32),
        k1 * c_n - k2 * s_n,
        k1 * s_n + k2 * c_n,
        jnp.zeros((ts, QK_PAD - QK_NOPE - QK_ROPE), F32),
    ], axis=-1)
    for hd in range(MLA_HEADS):
        k_ref[0, hd] = (kv_k[:, hd * QK_PAD:(hd + 1) * QK_PAD] + kr128).astype(BF16)

    v_t = _dot_nt(wv_t_ref[...], ckvn)
    for hd in range(MLA_HEADS):
        vt_ref[0, hd, 0] = v_t[hd * V_HEAD:(hd + 1) * V_HEAD].astype(BF16)


def _mla_proj_call(x, mod, pre_g, w_in, q_norm, kv_norm, w_uq, w_ukv, cn, sn, ct, st):
    bsz, seq, d = x.shape
    ts = SEQ_TILE
    nt = seq // ts
    half = QK_ROPE // 2
    n_c = Q_LORA + KV_LORA + QK_ROPE
    w_c = jnp.pad(w_in[:, :n_c], ((0, 0), (0, (-n_c) % LANE))).astype(BF16)
    w_z = w_in[:, n_c:].astype(BF16)
    wq = w_uq.reshape(Q_LORA, MLA_HEADS, QK_NOPE + QK_ROPE)
    wq = jnp.pad(wq, ((0, 0), (0, 0), (0, QK_PAD - QK_NOPE - QK_ROPE)))
    wuq_t = wq.reshape(Q_LORA, MLA_HEADS * QK_PAD).T.astype(BF16)
    wkv = w_ukv.reshape(KV_LORA, MLA_HEADS, QK_NOPE + V_HEAD)
    wk = jnp.pad(wkv[:, :, :QK_NOPE], ((0, 0), (0, 0), (0, QK_PAD - QK_NOPE)))
    wk = wk.reshape(KV_LORA, MLA_HEADS * QK_PAD).astype(BF16)
    wv_t = wkv[:, :, QK_NOPE:].reshape(KV_LORA, MLA_HEADS * V_HEAD).T.astype(BF16)

    tile = pl.BlockSpec((1, ts, d), lambda b, s: (b, s, 0))
    return pl.pallas_call(
        _mla_proj_kernel,
        grid=(bsz, nt),
        in_specs=[
            tile,
            pl.BlockSpec((1, 3, d), lambda b, s: (b, 0, 0)),
            _const_spec((1, d)),
            _const_spec(w_c.shape), _const_spec(w_z.shape),
            _const_spec((1, Q_LORA)), _const_spec((1, KV_LORA)),
            _const_spec(wuq_t.shape), _const_spec(wk.shape), _const_spec(wv_t.shape),
            pl.BlockSpec((1, ts, half), lambda b, s: (b, s, 0)),
            pl.BlockSpec((1, ts, half), lambda b, s: (b, s, 0)),
            pl.BlockSpec((1, half, ts), lambda b, s: (b, 0, s)),
            pl.BlockSpec((1, half, ts), lambda b, s: (b, 0, s)),
        ],
        out_specs=[
            pl.BlockSpec((1, MLA_HEADS, 1, QK_PAD, ts), lambda b, s: (b, 0, s, 0, 0)),
            pl.BlockSpec((1, MLA_HEADS, ts, QK_PAD), lambda b, s: (b, 0, s, 0)),
            pl.BlockSpec((1, MLA_HEADS, 1, V_HEAD, ts), lambda b, s: (b, 0, s, 0, 0)),
            tile,
        ],
        out_shape=[
            jax.ShapeDtypeStruct((bsz, MLA_HEADS, nt, QK_PAD, ts), BF16),
            jax.ShapeDtypeStruct((bsz, MLA_HEADS, seq, QK_PAD), BF16),
            jax.ShapeDtypeStruct((bsz, MLA_HEADS, nt, V_HEAD, ts), BF16),
            jax.ShapeDtypeStruct((bsz, seq, d), BF16),
        ],
        scratch_shapes=[pltpu.VMEM((ts, d), BF16)],
        compiler_params=_params(2),
        name="mla_proj",
    )(x, mod, pre_g.reshape(1, d), w_c, w_z, q_norm.reshape(1, Q_LORA), kv_norm.reshape(1, KV_LORA),
      wuq_t, wk, wv_t, cn, sn, ct, st)


def _mla_attn_kernel(qt_ref, k_ref, vt_ref, sz_ref, o_ref):
    tq = qt_ref.shape[-1]
    i = pl.program_id(2)

    def step(j, carry, masked):
        new = []
        for hh in range(HEAD_PAIR):
            m, l, acc = carry[hh]
            kb = k_ref[0, hh, pl.ds(pl.multiple_of(j * tq, tq), tq), :]
            st = _dot(kb, qt_ref[0, hh, 0])
            if masked:
                key = lax.broadcasted_iota(jnp.int32, st.shape, 0)
                qry = lax.broadcasted_iota(jnp.int32, st.shape, 1)
                st = jnp.where(key <= qry, st, NEG_BIG)
            m_new = jnp.maximum(m, jnp.max(st, axis=0, keepdims=True))
            alpha = jnp.exp(m - m_new)
            p = jnp.exp(st - m_new)
            l = alpha * l + jnp.sum(p, axis=0, keepdims=True)
            acc = alpha * acc + _dot(vt_ref[0, hh, j], p.astype(BF16))
            new.append((m_new, l, acc))
        return tuple(new)

    init = tuple((jnp.full((1, tq), NEG_BIG, F32), jnp.zeros((1, tq), F32),
                  jnp.zeros((V_HEAD, tq), F32)) for _ in range(HEAD_PAIR))
    carry = lax.fori_loop(0, i, lambda j, c: step(j, c, False), init)
    carry = step(i, carry, True)
    o_t = jnp.concatenate([acc / l for (_, l, acc) in carry], axis=0)
    o_ref[0] = (o_t.T * sz_ref[0].astype(F32)).astype(BF16)


def _mla_attn_call(qt, k, vt, sz):
    bsz, heads, nt, _, ts = qt.shape
    seq = nt * ts
    wide = HEAD_PAIR * V_HEAD
    return pl.pallas_call(
        _mla_attn_kernel,
        grid=(bsz, heads // HEAD_PAIR, nt),
        in_specs=[
            pl.BlockSpec((1, HEAD_PAIR, 1, QK_PAD, ts), lambda b, p, i: (b, p, i, 0, 0)),
            pl.BlockSpec((1, HEAD_PAIR, seq, QK_PAD), lambda b, p, i: (b, p, 0, 0)),
            pl.BlockSpec((1, HEAD_PAIR, nt, V_HEAD, ts), lambda b, p, i: (b, p, 0, 0, 0)),
            pl.BlockSpec((1, ts, wide), lambda b, p, i: (b, i, p)),
        ],
        out_specs=pl.BlockSpec((1, ts, wide), lambda b, p, i: (b, i, p)),
        out_shape=jax.ShapeDtypeStruct((bsz, seq, heads * V_HEAD), BF16),
        compiler_params=_params(3),
        name="mla_attn",
    )(qt, k, vt, sz)


def _mla_out_kernel(x_ref, o_ref_in, mod_ref, post_g_ref, w_out_ref, out_ref):
    y = _dot(o_ref_in[0], w_out_ref[...])
    out_ref[0] = _postnorm_residual(x_ref[0], y, mod_ref, post_g_ref)


def _mla_out_call(x, o, mod, post_g, w_out):
    bsz, seq, d = x.shape
    ts = SEQ_TILE
    tile = pl.BlockSpec((1, ts, d), lambda b, s: (b, s, 0))
    return pl.pallas_call(
        _mla_out_kernel,
        grid=(bsz, seq // ts),
        in_specs=[
            tile, tile,
            pl.BlockSpec((1, 3, d), lambda b, s: (b, 0, 0)),
            _const_spec((1, d)),
            _const_spec(w_out.shape),
        ],
        out_specs=tile,
        out_shape=jax.ShapeDtypeStruct(x.shape, F32),
        compiler_params=_params(2),
        name="mla_out",
    )(x, o, mod, post_g.reshape(1, d), w_out.astype(BF16))


def kernel(x, c, positions, ada_w, ada_b, pre_g, post_g, ev_w_in, ev_conv_w, ev_conv_b, ev_ln_g, ev_ln_b, ev_lru_conv_w, ev_lru_conv_b, ev_lru_wa, ev_lru_ba, ev_lru_wx, ev_lru_bx, ev_lru_lam, ev_w_out, od_w_in, od_q_norm, od_kv_norm, od_w_uq, od_w_ukv, od_w_out):
    depth = ada_w.shape[0]
    bsz, seq, d = x.shape
    assert d == D_MODEL and seq % SEQ_TILE == 0
    mod = _ada_call(c, ada_w, ada_b).reshape(depth, bsz, 3, d)
    cn, sn, ct, st = _rope_call(positions)
    for layer in range(depth):
        j = layer // 2
        if layer % 2 == 0:
            x = _even_call(x, mod[layer], pre_g[layer], post_g[layer], ev_w_in[j], ev_conv_w[j],
                           ev_conv_b[j], ev_ln_g[j], ev_ln_b[j], ev_lru_conv_w[j], ev_lru_conv_b[j],
                           ev_lru_wa[j], ev_lru_ba[j], ev_lru_wx[j], ev_lru_bx[j], ev_lru_lam[j],
                           ev_w_out[j])
        else:
            qt, k, vt, sz = _mla_proj_call(x, mod[layer], pre_g[layer], od_w_in[j], od_q_norm[j],
                                           od_kv_norm[j], od_w_uq[j], od_w_ukv[j], cn, sn, ct, st)
            o = _mla_attn_call(qt, k, vt, sz)
            x = _mla_out_call(x, o, mod[layer], post_g[layer], od_w_out[j])
    return x
```

```python
import functools

import jax
import jax.numpy as jnp
from jax import lax
from jax.experimental import pallas as pl
from jax.experimental.pallas import tpu as pltpu

F32 = jnp.float32
BF16 = jnp.bfloat16

D_MODEL = 1024
CONV_KERNEL = 31
LRU_CONV_KERNEL = 4
LRU_HEADS = 16
LRU_HEAD_DIM = D_MODEL // LRU_HEADS
LRU_C = 8.0
MLA_HEADS = 16
QK_NOPE = 64
QK_ROPE = 32
V_HEAD = 64
Q_LORA = 256
KV_LORA = 256
ROPE_THETA = 10000.0
EPS = 1e-6

LANE = 128
SUBLANE = 8
MXU_DIM = 256

SEQ_TILE = 256
N_SLAB = D_MODEL // LANE
CONV_HALO = 32
LRU_HALO = 8
CONV_TAPS_PAD = 32
GATE_GROUP = MXU_DIM
N_GATE_GROUP = D_MODEL // GATE_GROUP
CONV_ROWS = 64
QK_PAD = LANE
HEAD_GROUP = 4
MAX_PART = 32
V_AUG = V_HEAD + 16
LOG2_E = 1.4426950408889634
NEG_BIG = -1e30

VMEM_LIMIT_BYTES = 56 * 1024 * 1024


def _dot(a, b):
    return jnp.dot(a, b, preferred_element_type=F32)


def _dot_nt(a, b):
    return lax.dot_general(a, b, (((1,), (1,)), ((), ())), preferred_element_type=F32)


def _sigmoid(x):
    return 0.5 * jnp.tanh(0.5 * x) + 0.5


def _silu(x):
    h = 0.5 * x
    return h + h * jnp.tanh(h)


def _const_spec(shape):
    nd = len(shape)
    return pl.BlockSpec(shape, lambda *_: (0,) * nd, pipeline_mode=pl.Buffered(1))


def _params(n_grid):
    return pltpu.CompilerParams(
        dimension_semantics=("arbitrary",) * n_grid, vmem_limit_bytes=VMEM_LIMIT_BYTES)


def _ada_kernel(c_ref, w_ref, b_ref, o_ref):
    c = c_ref[...]
    o_ref[0] = jnp.dot(_silu(c), w_ref[0], preferred_element_type=F32,
                       precision=lax.Precision.HIGHEST) + b_ref[0]


def _ada_call(c, ada_w, ada_b):
    depth, d, d3 = ada_w.shape
    bsz = c.shape[0]
    n_col = d3 // d
    return pl.pallas_call(
        _ada_kernel,
        grid=(depth, n_col),
        in_specs=[
            pl.BlockSpec((bsz, d), lambda l, j: (0, 0)),
            pl.BlockSpec((1, d, d), lambda l, j: (l, 0, j)),
            pl.BlockSpec((1, 1, d), lambda l, j: (l, 0, j)),
        ],
        out_specs=pl.BlockSpec((1, bsz, d), lambda l, j: (l, 0, j)),
        out_shape=jax.ShapeDtypeStruct((depth, bsz, d3), F32),
        compiler_params=_params(2),
        name="ada_mod",
    )(c, ada_w, ada_b.reshape(depth, 1, d3))


def _rope_kernel(posrep_ref, invrep_ref, pos_ref, invcol_ref, cn_ref, sn_ref, ct_ref, st_ref):
    ang_n = posrep_ref[0].astype(F32) * invrep_ref[...]
    cn_ref[0] = jnp.cos(ang_n)
    sn_ref[0] = jnp.sin(ang_n)
    ang_t = invcol_ref[...] * pos_ref[0].astype(F32)
    ct_ref[0] = jnp.cos(ang_t)
    st_ref[0] = jnp.sin(ang_t)


def _rope_call(positions):
    bsz, seq = positions.shape
    half = QK_ROPE // 2
    rep = LANE // half
    inv = ROPE_THETA ** (-jnp.arange(0, QK_ROPE, 2, dtype=F32) / QK_ROPE)
    posrep = jnp.repeat(positions, half, axis=1).reshape(bsz, seq // rep, LANE)
    invrep = jnp.tile(inv, rep).reshape(1, LANE)
    n_spec = pl.BlockSpec((1, seq // rep, LANE), lambda b: (b, 0, 0))
    t_spec = pl.BlockSpec((1, half, seq), lambda b: (b, 0, 0))
    cn, sn, ct, st = pl.pallas_call(
        _rope_kernel,
        grid=(bsz,),
        in_specs=[
            n_spec,
            pl.BlockSpec((1, LANE), lambda b: (0, 0)),
            pl.BlockSpec((1, 1, seq), lambda b: (b, 0, 0)),
            pl.BlockSpec((half, 1), lambda b: (0, 0)),
        ],
        out_specs=[n_spec, n_spec, t_spec, t_spec],
        out_shape=[jax.ShapeDtypeStruct((bsz, seq // rep, LANE), F32)] * 2
        + [jax.ShapeDtypeStruct((bsz, half, seq), F32)] * 2,
        compiler_params=_params(1),
        name="rope_tables",
    )(posrep, invrep, positions.reshape(bsz, 1, seq), inv.reshape(half, 1))
    return cn.reshape(bsz, seq, half), sn.reshape(bsz, seq, half), ct, st


def _prenorm(x, mod_ref, pre_g_ref):
    ms = jnp.mean(x * x, axis=-1, keepdims=True)
    shift = mod_ref[0, 0:1, :]
    scale = mod_ref[0, 1:2, :]
    gmul = pre_g_ref[...] * (1.0 + scale)
    return x * lax.rsqrt(ms + EPS) * gmul + shift


def _postnorm_residual(x, y, mod_ref, post_g_ref):
    ms = jnp.mean(y * y, axis=-1, keepdims=True)
    gate = mod_ref[0, 2:3, :]
    return x + (gate * post_g_ref[...]) * (y * lax.rsqrt(ms + EPS))


def _scan_slab(a, b, carry):
    t = a.shape[0]
    groups = t // SUBLANE
    a = a.reshape(groups, SUBLANE, LANE)
    b = b.reshape(groups, SUBLANE, LANE)
    row = lax.broadcasted_iota(jnp.int32, a.shape, 1)
    d = 1
    while d < SUBLANE:
        ok = row >= d
        b = jnp.where(ok, a, 0.0) * pltpu.roll(b, d, axis=1) + b
        a = a * jnp.where(ok, pltpu.roll(a, d, axis=1), 1.0)
        d *= 2
    outs = []
    c = carry
    for v in range(groups):
        hv = b[v] + a[v] * c
        outs.append(hv)
        c = jnp.broadcast_to(hv[SUBLANE - 1:SUBLANE, :], (SUBLANE, LANE))
    return jnp.concatenate(outs, axis=0), c


def _even_kernel(x_ref, mod_ref, pre_g_ref, post_g_ref, w_vg_ref, w_rest_ref, cw_ref, cb_ref, lng_ref,
                 lnb_ref, lcw_ref, lcb_ref, wg_ref, ba_ref, bx_ref, lam_ref, w_out_ref, o_ref,
                 hbf, u0, u1, u2, ur, cbuf, cv, lbuf, xcf, xcb, a_s, b_s, hcar, act):
    ts = x_ref.shape[1]
    d = D_MODEL
    gw = GATE_GROUP
    spg = gw // LANE

    def rest(kind, c):
        lo = kind * gw + (c % spg) * LANE
        return ur[c // spg, :, lo:lo + LANE]

    @pl.when(pl.program_id(1) == 0)
    def _():
        cbuf[:, 0:CONV_HALO, :] = jnp.zeros((N_SLAB, CONV_HALO, LANE), F32)
        lbuf[:, 0:LRU_HALO, :] = jnp.zeros((N_SLAB, LRU_HALO, LANE), F32)
        hcar[...] = jnp.zeros(hcar.shape, F32)

    hbf[...] = _prenorm(x_ref[0], mod_ref, pre_g_ref).astype(BF16)

    u0[...] = _dot(hbf[...], w_vg_ref[:, 0:d])
    u1[...] = _dot(hbf[...], w_vg_ref[:, d:2 * d])
    u2[...] = _dot(hbf[...], w_vg_ref[:, 2 * d:3 * d])
    for c in range(N_SLAB):
        sl = slice(c * LANE, (c + 1) * LANE)
        cbuf[c, CONV_HALO:CONV_HALO + ts, :] = u0[:, sl] * _sigmoid(u1[:, sl])

    def conv_slab(c):
        w = cw_ref[c]
        for r0 in range(0, ts, CONV_ROWS):
            acc = jnp.broadcast_to(cb_ref[c], (CONV_ROWS, LANE))
            for j in range(CONV_KERNEL):
                start = CONV_HALO - (CONV_KERNEL - 1) + j + r0
                acc = acc + cbuf[c, start:start + CONV_ROWS, :] * w[j:j + 1, :]
            cv[c, r0:r0 + CONV_ROWS, :] = acc

    def conv_group(g, carry):
        ur[g] = _dot(hbf[...], w_rest_ref[g])
        for q in range(spg):
            conv_slab(spg * g + q)
        return carry

    lax.fori_loop(0, N_GATE_GROUP, conv_group, 0)

    tot = cv[0]
    for c in range(1, N_SLAB):
        tot = tot + cv[c]
    mu = jnp.broadcast_to(jnp.sum(tot, axis=-1, keepdims=True) * (1.0 / d), (ts, LANE))
    sq = jnp.zeros((ts, LANE), F32)
    for c in range(N_SLAB):
        dc = cv[c] - mu
        sq = sq + dc * dc
    rstd = lax.rsqrt(jnp.sum(sq, axis=-1, keepdims=True) * (1.0 / d) + EPS)
    rstd = jnp.broadcast_to(rstd, (ts, LANE))
    for c in range(N_SLAB):
        sl = slice(c * LANE, (c + 1) * LANE)
        yn = (cv[c] - mu) * rstd * lng_ref[:, sl] + lnb_ref[:, sl]
        act[:, sl] = (_silu(yn) * _silu(rest(0, c))).astype(BF16)

    for c in range(N_SLAB):
        sl = slice(c * LANE, (c + 1) * LANE)
        lbuf[c, LRU_HALO:LRU_HALO + ts, :] = u2[:, sl]
        acc = jnp.broadcast_to(lcb_ref[:, sl], (ts, LANE))
        for j in range(LRU_CONV_KERNEL):
            start = LRU_HALO - (LRU_CONV_KERNEL - 1) + j
            acc = acc + lbuf[c, start:start + ts, :] * lcw_ref[j:j + 1, sl]
        xcf[:, sl] = acc
        xcb[:, sl] = acc.astype(BF16)

    nl = -lam_ref[...]
    softplus = jnp.maximum(nl, 0.0) + jnp.log1p(jnp.exp(-jnp.abs(nl)))
    for g in range(N_GATE_GROUP):
        sl = slice(g * gw, (g + 1) * gw)
        rg = _dot(xcb[:, sl], wg_ref[g])
        r = _sigmoid(rg[:, 0:gw] + ba_ref[:, sl])
        i = _sigmoid(rg[:, gw:2 * gw] + bx_ref[:, sl])
        a = jnp.exp((-LRU_C) * r * softplus[:, sl])
        a_s[:, sl] = a
        b_s[:, sl] = jnp.sqrt(1.0 - a * a) * i * xcf[:, sl]

    for c in range(N_SLAB):
        sl = slice(c * LANE, (c + 1) * LANE)
        h, c_new = _scan_slab(a_s[:, sl], b_s[:, sl], hcar[:, sl])
        hcar[:, sl] = c_new
        act[:, d + c * LANE:d + (c + 1) * LANE] = (h * _silu(rest(1, c))).astype(BF16)

    cbuf[:, 0:CONV_HALO, :] = cbuf[:, ts:ts + CONV_HALO, :]
    lbuf[:, 0:LRU_HALO, :] = lbuf[:, ts:ts + LRU_HALO, :]

    y = _dot(act[...], w_out_ref[...])
    o_ref[0] = _postnorm_residual(x_ref[0], y, mod_ref, post_g_ref)


def _even_call(x, mod, pre_g, post_g, w_in, conv_w, conv_b, ln_g, ln_b, lcw, lcb,
               wa, ba, wx, bx, lam, w_out):
    bsz, seq, d = x.shape
    ts = SEQ_TILE
    row = lambda v: v.reshape(1, d)
    cw = jnp.pad(conv_w, ((0, CONV_TAPS_PAD - CONV_KERNEL), (0, 0)))
    cw = cw.reshape(CONV_TAPS_PAD, N_SLAB, LANE).transpose(1, 0, 2)
    cb = conv_b.reshape(N_SLAB, 1, LANE)
    hpg = GATE_GROUP // LRU_HEAD_DIM
    eye = jnp.eye(hpg, dtype=F32)

    def blockdiag(w):
        wg = w.reshape(N_GATE_GROUP, hpg, LRU_HEAD_DIM, LRU_HEAD_DIM)
        return jnp.einsum('ghij,hk->ghikj', wg, eye).reshape(N_GATE_GROUP, GATE_GROUP, GATE_GROUP)

    wg = jnp.concatenate([blockdiag(wa), blockdiag(wx)], axis=-1).astype(BF16)
    ng, gw = N_GATE_GROUP, GATE_GROUP
    w_vg = jnp.concatenate([w_in[:, 0:2 * d], w_in[:, 3 * d:4 * d]], axis=1).astype(BF16)
    grouped = lambda lo: w_in[:, lo * d:(lo + 1) * d].reshape(d, ng, gw)
    w_rest = jnp.concatenate([grouped(2), grouped(4)], axis=2)
    w_rest = w_rest.transpose(1, 0, 2).astype(BF16)

    tile = pl.BlockSpec((1, ts, d), lambda b, s: (b, s, 0))
    scratch = [
        pltpu.VMEM((ts, d), BF16),
        pltpu.VMEM((ts, d), F32),
        pltpu.VMEM((ts, d), F32),
        pltpu.VMEM((ts, d), F32),
        pltpu.VMEM((ng, ts, 2 * gw), F32),
        pltpu.VMEM((N_SLAB, CONV_HALO + ts, LANE), F32),
        pltpu.VMEM((N_SLAB, ts, LANE), F32),
        pltpu.VMEM((N_SLAB, LRU_HALO + ts, LANE), F32),
        pltpu.VMEM((ts, d), F32),
        pltpu.VMEM((ts, d), BF16),
        pltpu.VMEM((ts, d), F32),
        pltpu.VMEM((ts, d), F32),
        pltpu.VMEM((SUBLANE, d), F32),
        pltpu.VMEM((ts, 2 * d), BF16),
    ]
    return pl.pallas_call(
        _even_kernel,
        grid=(bsz, seq // ts),
        in_specs=[
            tile,
            pl.BlockSpec((1, 3, d), lambda b, s: (b, 0, 0)),
            _const_spec((1, d)), _const_spec((1, d)),
            _const_spec(w_vg.shape), _const_spec(w_rest.shape),
            _const_spec(cw.shape), _const_spec(cb.shape),
            _const_spec((1, d)), _const_spec((1, d)),
            _const_spec(lcw.shape), _const_spec((1, d)),
            _const_spec(wg.shape),
            _const_spec((1, d)), _const_spec((1, d)), _const_spec((1, d)),
            _const_spec(w_out.shape),
        ],
        out_specs=tile,
        out_shape=jax.ShapeDtypeStruct(x.shape, F32),
        scratch_shapes=scratch,
        compiler_params=_params(2),
        name="even_layer",
    )(x, mod, row(pre_g), row(post_g), w_vg, w_rest, cw, cb, row(ln_g), row(ln_b),
      lcw, row(lcb), wg, row(ba), row(bx), row(lam), w_out.astype(BF16))


def _mla_proj_kernel(x_ref, mod_ref, pre_g_ref, w_c_ref, w_z_ref, qn_ref, kvn_ref,
                     wuq_t_ref, wk_ref, wv_t_ref, cn_ref, sn_ref, ct_ref, st_ref,
                     qt_ref, k_ref, vt_ref, sz_ref, hbf):
    ts = x_ref.shape[1]
    half = QK_ROPE // 2
    hbf[...] = _prenorm(x_ref[0], mod_ref, pre_g_ref).astype(BF16)

    uc = _dot(hbf[...], w_c_ref[...])
    sz_ref[0] = _silu(_dot(hbf[...], w_z_ref[...])).astype(BF16)

    def rms(v, g_ref):
        ms = jnp.mean(v * v, axis=-1, keepdims=True)
        return (v * lax.rsqrt(ms + EPS) * g_ref[...]).astype(BF16)

    cqn = rms(uc[:, 0:Q_LORA], qn_ref)
    ckvn = rms(uc[:, Q_LORA:Q_LORA + KV_LORA], kvn_ref)
    kr = uc[:, Q_LORA + KV_LORA:Q_LORA + KV_LORA + QK_ROPE]

    q_t = _dot_nt(wuq_t_ref[...], cqn)
    c_t = ct_ref[0]
    s_t = st_ref[0]
    scale = (QK_NOPE + QK_ROPE) ** -0.5 * LOG2_E
    for hd in range(MLA_HEADS):
        base = hd * QK_PAD
        x1 = q_t[base + QK_NOPE:base + QK_NOPE + half]
        x2 = q_t[base + QK_NOPE + half:base + QK_NOPE + QK_ROPE]
        blk = jnp.concatenate([
            q_t[base:base + QK_NOPE],
            x1 * c_t - x2 * s_t,
            x1 * s_t + x2 * c_t,
            q_t[base + QK_NOPE + QK_ROPE:base + QK_PAD],
        ], axis=0)
        qt_ref[0, hd, 0] = (blk * scale).astype(BF16)

    kv_k = _dot(ckvn, wk_ref[...])
    c_n = cn_ref[0]
    s_n = sn_ref[0]
    k1 = kr[:, 0:half]
    k2 = kr[:, half:QK_ROPE]
    kr128 = jnp.concatenate([
        jnp.zeros((ts, QK_NOPE), F32),
        k1 * c_n - k2 * s_n,
        k1 * s_n + k2 * c_n,
        jnp.zeros((ts, QK_PAD - QK_NOPE - QK_ROPE), F32),
    ], axis=-1)
    for hd in range(MLA_HEADS):
        k_ref[0, hd] = (kv_k[:, hd * QK_PAD:(hd + 1) * QK_PAD] + kr128).astype(BF16)

    v_t = _dot_nt(wv_t_ref[...], ckvn)
    for hd in range(MLA_HEADS):
        vt_ref[0, hd, 0:V_HEAD, :] = v_t[hd * V_HEAD:(hd + 1) * V_HEAD].astype(BF16)
        row = lax.broadcasted_iota(jnp.int32, (V_AUG - V_HEAD, ts), 0)
        vt_ref[0, hd, V_HEAD:V_AUG, :] = jnp.where(row == 0, 1.0, 0.0).astype(BF16)


def _mla_proj_call(x, mod, pre_g, w_in, q_norm, kv_norm, w_uq, w_ukv, cn, sn, ct, st):
    bsz, seq, d = x.shape
    ts = SEQ_TILE
    nt = seq // ts
    half = QK_ROPE // 2
    n_c = Q_LORA + KV_LORA + QK_ROPE
    w_c = jnp.pad(w_in[:, :n_c], ((0, 0), (0, (-n_c) % LANE))).astype(BF16)
    w_z = w_in[:, n_c:].astype(BF16)
    wq = w_uq.reshape(Q_LORA, MLA_HEADS, QK_NOPE + QK_ROPE)
    wq = jnp.pad(wq, ((0, 0), (0, 0), (0, QK_PAD - QK_NOPE - QK_ROPE)))
    wuq_t = wq.reshape(Q_LORA, MLA_HEADS * QK_PAD).T.astype(BF16)
    wkv = w_ukv.reshape(KV_LORA, MLA_HEADS, QK_NOPE + V_HEAD)
    wk = jnp.pad(wkv[:, :, :QK_NOPE], ((0, 0), (0, 0), (0, QK_PAD - QK_NOPE)))
    wk = wk.reshape(KV_LORA, MLA_HEADS * QK_PAD).astype(BF16)
    wv_t = wkv[:, :, QK_NOPE:].reshape(KV_LORA, MLA_HEADS * V_HEAD).T.astype(BF16)

    tile = pl.BlockSpec((1, ts, d), lambda b, s: (b, s, 0))
    return pl.pallas_call(
        _mla_proj_kernel,
        grid=(bsz, nt),
        in_specs=[
            tile,
            pl.BlockSpec((1, 3, d), lambda b, s: (b, 0, 0)),
            _const_spec((1, d)),
            _const_spec(w_c.shape), _const_spec(w_z.shape),
            _const_spec((1, Q_LORA)), _const_spec((1, KV_LORA)),
            _const_spec(wuq_t.shape), _const_spec(wk.shape), _const_spec(wv_t.shape),
            pl.BlockSpec((1, ts, half), lambda b, s: (b, s, 0)),
            pl.BlockSpec((1, ts, half), lambda b, s: (b, s, 0)),
            pl.BlockSpec((1, half, ts), lambda b, s: (b, 0, s)),
            pl.BlockSpec((1, half, ts), lambda b, s: (b, 0, s)),
        ],
        out_specs=[
            pl.BlockSpec((1, MLA_HEADS, 1, QK_PAD, ts), lambda b, s: (b, 0, s, 0, 0)),
            pl.BlockSpec((1, MLA_HEADS, ts, QK_PAD), lambda b, s: (b, 0, s, 0)),
            pl.BlockSpec((1, MLA_HEADS, V_AUG, ts), lambda b, s: (b, 0, 0, s)),
            tile,
        ],
        out_shape=[
            jax.ShapeDtypeStruct((bsz, MLA_HEADS, nt, QK_PAD, ts), BF16),
            jax.ShapeDtypeStruct((bsz, MLA_HEADS, seq, QK_PAD), BF16),
            jax.ShapeDtypeStruct((bsz, MLA_HEADS, V_AUG, seq), BF16),
            jax.ShapeDtypeStruct((bsz, seq, d), BF16),
        ],
        scratch_shapes=[pltpu.VMEM((ts, d), BF16)],
        compiler_params=_params(2),
        name="mla_proj",
    )(x, mod, pre_g.reshape(1, d), w_c, w_z, q_norm.reshape(1, Q_LORA), kv_norm.reshape(1, KV_LORA),
      wuq_t, wk, wv_t, cn, sn, ct, st)


def _mla_attn_kernel(qt_ref, k_ref, vt_ref, sz_ref, o_ref, s_even, s_odd, m_even, m_odd, p_scr):
    tq = qt_ref.shape[-1]
    nt = k_ref.shape[2] // tq
    t = pl.program_id(2)
    heads = range(HEAD_GROUP)

    def step(v):
        n_a = v + 1 if v < nt else 0
        n_b = v if v > 0 else 0
        s_a, m_a = (s_even, m_even) if v % 2 == 0 else (s_odd, m_odd)
        s_b, m_bref = (s_odd, m_odd) if v % 2 == 0 else (s_even, m_even)
        key = lax.broadcasted_iota(jnp.int32, (tq, tq), 0)
        qry = lax.broadcasted_iota(jnp.int32, (tq, tq), 1)
        q_t = [qt_ref[0, hh, 0] for hh in heads] if n_a else None
        m_b = [m_bref[hh] for hh in heads] if n_b else None
        m_part = [None] * HEAD_GROUP
        for j in range(max(n_a, n_b)):
            blk = slice(j * tq, (j + 1) * tq)
            if j < n_a:
                for hh in heads:
                    s = _dot(k_ref[0, hh, blk, :], q_t[hh])
                    if j == v:
                        s = jnp.where(key <= qry, s, NEG_BIG)
                    s_a[hh, blk, :] = s
                    pm = jnp.max(s.reshape(tq // MAX_PART, MAX_PART, tq), axis=0)
                    m_part[hh] = pm if m_part[hh] is None else jnp.maximum(m_part[hh], pm)
            if j < n_b:
                for hh in heads:
                    p = jnp.exp2(s_b[hh, blk, :] - m_b[hh]).astype(BF16)
                    p_scr[hh, blk, :] = p
            if n_b and j == n_b - 1:
                kend = n_b * tq
                outs = []
                for hh in heads:
                    o_aug = _dot(vt_ref[0, hh, :, 0:kend], p_scr[hh, 0:kend, :])
                    outs.append(o_aug[0:V_HEAD] / o_aug[V_HEAD:V_HEAD + 1])
                o = jnp.concatenate(outs, axis=0).T
                o_ref[0] = (o * sz_ref[0].astype(F32)).astype(BF16)
        if n_a:
            for hh in heads:
                m_a[hh] = jnp.max(m_part[hh], axis=0, keepdims=True)

    for v in range(nt + 1):
        pl.when(t == v)(functools.partial(step, v))


def _mla_attn_call(qt, k, vt, sz):
    bsz, heads, nt, _, ts = qt.shape
    seq = nt * ts
    wide = HEAD_GROUP * V_HEAD
    prev = lambda t: jnp.maximum(t - 1, 0)
    return pl.pallas_call(
        _mla_attn_kernel,
        grid=(bsz, heads // HEAD_GROUP, nt + 1),
        in_specs=[
            pl.BlockSpec((1, HEAD_GROUP, 1, QK_PAD, ts),
                         lambda b, p, t: (b, p, jnp.minimum(t, nt - 1), 0, 0)),
            pl.BlockSpec((1, HEAD_GROUP, seq, QK_PAD), lambda b, p, t: (b, p, 0, 0)),
            pl.BlockSpec((1, HEAD_GROUP, V_AUG, seq), lambda b, p, t: (b, p, 0, 0)),
            pl.BlockSpec((1, ts, wide), lambda b, p, t: (b, prev(t), p)),
        ],
        out_specs=pl.BlockSpec((1, ts, wide), lambda b, p, t: (b, prev(t), p)),
        out_shape=jax.ShapeDtypeStruct((bsz, seq, heads * V_HEAD), BF16),
        scratch_shapes=[pltpu.VMEM((HEAD_GROUP, seq, ts), F32),
                        pltpu.VMEM((HEAD_GROUP, seq, ts), F32),
                        pltpu.VMEM((HEAD_GROUP, 1, ts), F32),
                        pltpu.VMEM((HEAD_GROUP, 1, ts), F32),
                        pltpu.VMEM((HEAD_GROUP, seq, ts), BF16)],
        compiler_params=_params(3),
        name="mla_attn",
    )(qt, k, vt, sz)


def _mla_out_kernel(x_ref, o_ref_in, mod_ref, post_g_ref, w_out_ref, out_ref):
    y = _dot(o_ref_in[0], w_out_ref[...])
    out_ref[0] = _postnorm_residual(x_ref[0], y, mod_ref, post_g_ref)


def _mla_out_call(x, o, mod, post_g, w_out):
    bsz, seq, d = x.shape
    ts = SEQ_TILE
    tile = pl.BlockSpec((1, ts, d), lambda b, s: (b, s, 0))
    return pl.pallas_call(
        _mla_out_kernel,
        grid=(bsz, seq // ts),
        in_specs=[
            tile, tile,
            pl.BlockSpec((1, 3, d), lambda b, s: (b, 0, 0)),
            _const_spec((1, d)),
            _const_spec(w_out.shape),
        ],
        out_specs=tile,
        out_shape=jax.ShapeDtypeStruct(x.shape, F32),
        compiler_params=_params(2),
        name="mla_out",
    )(x, o, mod, post_g.reshape(1, d), w_out.astype(BF16))


def kernel(x, c, positions, ada_w, ada_b, pre_g, post_g, ev_w_in, ev_conv_w, ev_conv_b, ev_ln_g, ev_ln_b, ev_lru_conv_w, ev_lru_conv_b, ev_lru_wa, ev_lru_ba, ev_lru_wx, ev_lru_bx, ev_lru_lam, ev_w_out, od_w_in, od_q_norm, od_kv_norm, od_w_uq, od_w_ukv, od_w_out):
    depth = ada_w.shape[0]
    bsz, seq, d = x.shape
    assert d == D_MODEL and seq % SEQ_TILE == 0
    mod = _ada_call(c, ada_w, ada_b).reshape(depth, bsz, 3, d)
    cn, sn, ct, st = _rope_call(positions)
    for layer in range(depth):
        j = layer // 2
        if layer % 2 == 0:
            x = _even_call(x, mod[layer], pre_g[layer], post_g[layer], ev_w_in[j], ev_conv_w[j],
                           ev_conv_b[j], ev_ln_g[j], ev_ln_b[j], ev_lru_conv_w[j], ev_lru_conv_b[j],
                           ev_lru_wa[j], ev_lru_ba[j], ev_lru_wx[j], ev_lru_bx[j], ev_lru_lam[j],
                           ev_w_out[j])
        else:
            qt, k, vt, sz = _mla_proj_call(x, mod[layer], pre_g[layer], od_w_in[j], od_q_norm[j],
                                           od_kv_norm[j], od_w_uq[j], od_w_ukv[j], cn, sn, ct, st)
            o = _mla_attn_call(qt, k, vt, sz)
            x = _mla_out_call(x, o, mod[layer], post_g[layer], od_w_out[j])
    return x
```

```python
import functools

import jax
import jax.numpy as jnp
from jax import lax
from jax.experimental import pallas as pl
from jax.experimental.pallas import tpu as pltpu

F32 = jnp.float32
BF16 = jnp.bfloat16

D_MODEL = 1024
CONV_KERNEL = 31
LRU_CONV_KERNEL = 4
LRU_HEADS = 16
LRU_HEAD_DIM = D_MODEL // LRU_HEADS
LRU_C = 8.0
MLA_HEADS = 16
QK_NOPE = 64
QK_ROPE = 32
V_HEAD = 64
Q_LORA = 256
KV_LORA = 256
ROPE_THETA = 10000.0
EPS = 1e-6

LANE = 128
SUBLANE = 8
MXU_DIM = 256

SEQ_TILE = 256
N_SLAB = D_MODEL // LANE
CONV_HALO = 32
LRU_HALO = 8
CONV_TAPS_PAD = 32
GATE_GROUP = MXU_DIM
N_GATE_GROUP = D_MODEL // GATE_GROUP
CONV_ROWS = 64
QK_PAD = LANE
HEAD_GROUP = 4
MAX_PART = 32
V_AUG = V_HEAD + 16
LOG2_E = 1.4426950408889634
NEG_BIG = -1e30

VMEM_LIMIT_BYTES = 56 * 1024 * 1024


def _dot(a, b):
    return jnp.dot(a, b, preferred_element_type=F32)


def _dot_nt(a, b):
    return lax.dot_general(a, b, (((1,), (1,)), ((), ())), preferred_element_type=F32)


def _sigmoid(x):
    return 0.5 * jnp.tanh(0.5 * x) + 0.5


def _silu(x):
    h = 0.5 * x
    return h + h * jnp.tanh(h)


def _const_spec(shape):
    nd = len(shape)
    return pl.BlockSpec(shape, lambda *_: (0,) * nd, pipeline_mode=pl.Buffered(1))


def _params(n_grid):
    return pltpu.CompilerParams(
        dimension_semantics=("arbitrary",) * n_grid, vmem_limit_bytes=VMEM_LIMIT_BYTES)


def _ada_kernel(c_ref, w_ref, b_ref, o_ref):
    c = c_ref[...]
    o_ref[0] = jnp.dot(_silu(c), w_ref[0], preferred_element_type=F32,
                       precision=lax.Precision.HIGHEST) + b_ref[0]


def _ada_call(c, ada_w, ada_b):
    depth, d, d3 = ada_w.shape
    bsz = c.shape[0]
    n_col = d3 // d
    return pl.pallas_call(
        _ada_kernel,
        grid=(depth, n_col),
        in_specs=[
            pl.BlockSpec((bsz, d), lambda l, j: (0, 0)),
            pl.BlockSpec((1, d, d), lambda l, j: (l, 0, j)),
            pl.BlockSpec((1, 1, d), lambda l, j: (l, 0, j)),
        ],
        out_specs=pl.BlockSpec((1, bsz, d), lambda l, j: (l, 0, j)),
        out_shape=jax.ShapeDtypeStruct((depth, bsz, d3), F32),
        compiler_params=_params(2),
        name="ada_mod",
    )(c, ada_w, ada_b.reshape(depth, 1, d3))


def _rope_kernel(posrep_ref, invrep_ref, pos_ref, invcol_ref, cn_ref, sn_ref, ct_ref, st_ref):
    ang_n = posrep_ref[0].astype(F32) * invrep_ref[...]
    cn_ref[0] = jnp.cos(ang_n)
    sn_ref[0] = jnp.sin(ang_n)
    ang_t = invcol_ref[...] * pos_ref[0].astype(F32)
    ct_ref[0] = jnp.cos(ang_t)
    st_ref[0] = jnp.sin(ang_t)


def _rope_call(positions):
    bsz, seq = positions.shape
    half = QK_ROPE // 2
    rep = LANE // half
    inv = ROPE_THETA ** (-jnp.arange(0, QK_ROPE, 2, dtype=F32) / QK_ROPE)
    posrep = jnp.repeat(positions, half, axis=1).reshape(bsz, seq // rep, LANE)
    invrep = jnp.tile(inv, rep).reshape(1, LANE)
    n_spec = pl.BlockSpec((1, seq // rep, LANE), lambda b: (b, 0, 0))
    t_spec = pl.BlockSpec((1, half, seq), lambda b: (b, 0, 0))
    cn, sn, ct, st = pl.pallas_call(
        _rope_kernel,
        grid=(bsz,),
        in_specs=[
            n_spec,
            pl.BlockSpec((1, LANE), lambda b: (0, 0)),
            pl.BlockSpec((1, 1, seq), lambda b: (b, 0, 0)),
            pl.BlockSpec((half, 1), lambda b: (0, 0)),
        ],
        out_specs=[n_spec, n_spec, t_spec, t_spec],
        out_shape=[jax.ShapeDtypeStruct((bsz, seq // rep, LANE), F32)] * 2
        + [jax.ShapeDtypeStruct((bsz, half, seq), F32)] * 2,
        compiler_params=_params(1),
        name="rope_tables",
    )(posrep, invrep, positions.reshape(bsz, 1, seq), inv.reshape(half, 1))
    return cn.reshape(bsz, seq, half), sn.reshape(bsz, seq, half), ct, st


def _prenorm(x, mod_ref, pre_g_ref):
    ms = jnp.mean(x * x, axis=-1, keepdims=True)
    shift = mod_ref[0, 0:1, :]
    scale = mod_ref[0, 1:2, :]
    gmul = pre_g_ref[...] * (1.0 + scale)
    return x * lax.rsqrt(ms + EPS) * gmul + shift


def _postnorm_residual(x, y, mod_ref, post_g_ref):
    ms = jnp.mean(y * y, axis=-1, keepdims=True)
    gate = mod_ref[0, 2:3, :]
    return x + (gate * post_g_ref[...]) * (y * lax.rsqrt(ms + EPS))


def _scan_slab(a, b, carry):
    t = a.shape[0]
    groups = t // SUBLANE
    a = a.reshape(groups, SUBLANE, LANE)
    b = b.reshape(groups, SUBLANE, LANE)
    row = lax.broadcasted_iota(jnp.int32, a.shape, 1)
    d = 1
    while d < SUBLANE:
        ok = row >= d
        b = jnp.where(ok, a, 0.0) * pltpu.roll(b, d, axis=1) + b
        a = a * jnp.where(ok, pltpu.roll(a, d, axis=1), 1.0)
        d *= 2
    outs = []
    c = carry
    for v in range(groups):
        hv = b[v] + a[v] * c
        outs.append(hv)
        c = jnp.broadcast_to(hv[SUBLANE - 1:SUBLANE, :], (SUBLANE, LANE))
    return jnp.concatenate(outs, axis=0), c


def _even_kernel(x_ref, mod_ref, pre_g_ref, post_g_ref, w_vg_ref, w_rest_ref, cw_ref, cb_ref, lng_ref,
                 lnb_ref, lcw_ref, lcb_ref, wg_ref, ba_ref, bx_ref, lam_ref, w_out_ref, o_ref,
                 hbf, u0, u1, u2, ur, cbuf, cv, lbuf, xcf, xcb, a_s, b_s, hcar, act):
    ts = x_ref.shape[1]
    d = D_MODEL
    gw = GATE_GROUP
    spg = gw // LANE

    def rest(kind, c):
        lo = kind * gw + (c % spg) * LANE
        return ur[c // spg, :, lo:lo + LANE]

    @pl.when(pl.program_id(1) == 0)
    def _():
        cbuf[:, 0:CONV_HALO, :] = jnp.zeros((N_SLAB, CONV_HALO, LANE), F32)
        lbuf[:, 0:LRU_HALO, :] = jnp.zeros((N_SLAB, LRU_HALO, LANE), F32)
        hcar[...] = jnp.zeros(hcar.shape, F32)

    hbf[...] = _prenorm(x_ref[0], mod_ref, pre_g_ref).astype(BF16)

    u0[...] = _dot(hbf[...], w_vg_ref[:, 0:d])
    u1[...] = _dot(hbf[...], w_vg_ref[:, d:2 * d])
    u2[...] = _dot(hbf[...], w_vg_ref[:, 2 * d:3 * d])
    for g in range(N_GATE_GROUP):
        ur[g] = _dot(hbf[...], w_rest_ref[g])
    for c in range(N_SLAB):
        sl = slice(c * LANE, (c + 1) * LANE)
        cbuf[c, CONV_HALO:CONV_HALO + ts, :] = u0[:, sl] * _sigmoid(u1[:, sl])

    first = CONV_HALO - (CONV_KERNEL - 1)
    tap_groups = [[j for j in range(CONV_KERNEL) if (first + j) % SUBLANE == r] for r in range(SUBLANE)]

    def conv_slab(c, carry):
        w = cw_ref[c]
        for r0 in range(0, ts, CONV_ROWS):
            acc = jnp.broadcast_to(cb_ref[c], (CONV_ROWS, LANE))
            for taps in tap_groups:
                start = first + taps[0] + r0
                win = cbuf[c, start:start + CONV_ROWS + SUBLANE * (len(taps) - 1), :]
                for q, j in enumerate(taps):
                    acc = acc + win[SUBLANE * q:SUBLANE * q + CONV_ROWS] * w[j:j + 1, :]
            cv[c, r0:r0 + CONV_ROWS, :] = acc
        return carry

    lax.fori_loop(0, N_SLAB, conv_slab, 0)

    tot = cv[0]
    for c in range(1, N_SLAB):
        tot = tot + cv[c]
    mu = jnp.broadcast_to(jnp.sum(tot, axis=-1, keepdims=True) * (1.0 / d), (ts, LANE))
    sq = jnp.zeros((ts, LANE), F32)
    for c in range(N_SLAB):
        dc = cv[c] - mu
        sq = sq + dc * dc
    rstd = lax.rsqrt(jnp.sum(sq, axis=-1, keepdims=True) * (1.0 / d) + EPS)
    rstd = jnp.broadcast_to(rstd, (ts, LANE))
    for c in range(N_SLAB):
        sl = slice(c * LANE, (c + 1) * LANE)
        yn = (cv[c] - mu) * rstd * lng_ref[:, sl] + lnb_ref[:, sl]
        act[:, sl] = (_silu(yn) * _silu(rest(0, c))).astype(BF16)

    for c in range(N_SLAB):
        sl = slice(c * LANE, (c + 1) * LANE)
        lbuf[c, LRU_HALO:LRU_HALO + ts, :] = u2[:, sl]
        acc = jnp.broadcast_to(lcb_ref[:, sl], (ts, LANE))
        for j in range(LRU_CONV_KERNEL):
            start = LRU_HALO - (LRU_CONV_KERNEL - 1) + j
            acc = acc + lbuf[c, start:start + ts, :] * lcw_ref[j:j + 1, sl]
        xcf[:, sl] = acc
        xcb[:, sl] = acc.astype(BF16)

    nl = -lam_ref[...]
    softplus = jnp.maximum(nl, 0.0) + jnp.log1p(jnp.exp(-jnp.abs(nl)))
    for g in range(N_GATE_GROUP):
        sl = slice(g * gw, (g + 1) * gw)
        rg = _dot(xcb[:, sl], wg_ref[g])
        r = _sigmoid(rg[:, 0:gw] + ba_ref[:, sl])
        i = _sigmoid(rg[:, gw:2 * gw] + bx_ref[:, sl])
        a = jnp.exp((-LRU_C) * r * softplus[:, sl])
        a_s[:, sl] = a
        b_s[:, sl] = jnp.sqrt(1.0 - a * a) * i * xcf[:, sl]

    for c in range(N_SLAB):
        sl = slice(c * LANE, (c + 1) * LANE)
        h, c_new = _scan_slab(a_s[:, sl], b_s[:, sl], hcar[:, sl])
        hcar[:, sl] = c_new
        act[:, d + c * LANE:d + (c + 1) * LANE] = (h * _silu(rest(1, c))).astype(BF16)

    cbuf[:, 0:CONV_HALO, :] = cbuf[:, ts:ts + CONV_HALO, :]
    lbuf[:, 0:LRU_HALO, :] = lbuf[:, ts:ts + LRU_HALO, :]

    y = _dot(act[...], w_out_ref[...])
    o_ref[0] = _postnorm_residual(x_ref[0], y, mod_ref, post_g_ref)


def _even_call(x, mod, pre_g, post_g, w_in, conv_w, conv_b, ln_g, ln_b, lcw, lcb,
               wa, ba, wx, bx, lam, w_out):
    bsz, seq, d = x.shape
    ts = SEQ_TILE
    row = lambda v: v.reshape(1, d)
    cw = jnp.pad(conv_w, ((0, CONV_TAPS_PAD - CONV_KERNEL), (0, 0)))
    cw = cw.reshape(CONV_TAPS_PAD, N_SLAB, LANE).transpose(1, 0, 2)
    cb = conv_b.reshape(N_SLAB, 1, LANE)
    hpg = GATE_GROUP // LRU_HEAD_DIM
    eye = jnp.eye(hpg, dtype=F32)

    def blockdiag(w):
        wg = w.reshape(N_GATE_GROUP, hpg, LRU_HEAD_DIM, LRU_HEAD_DIM)
        return jnp.einsum('ghij,hk->ghikj', wg, eye).reshape(N_GATE_GROUP, GATE_GROUP, GATE_GROUP)

    wg = jnp.concatenate([blockdiag(wa), blockdiag(wx)], axis=-1).astype(BF16)
    ng, gw = N_GATE_GROUP, GATE_GROUP
    w_vg = jnp.concatenate([w_in[:, 0:2 * d], w_in[:, 3 * d:4 * d]], axis=1).astype(BF16)
    grouped = lambda lo: w_in[:, lo * d:(lo + 1) * d].reshape(d, ng, gw)
    w_rest = jnp.concatenate([grouped(2), grouped(4)], axis=2)
    w_rest = w_rest.transpose(1, 0, 2).astype(BF16)

    tile = pl.BlockSpec((1, ts, d), lambda b, s: (b, s, 0))
    scratch = [
        pltpu.VMEM((ts, d), BF16),
        pltpu.VMEM((ts, d), F32),
        pltpu.VMEM((ts, d), F32),
        pltpu.VMEM((ts, d), F32),
        pltpu.VMEM((ng, ts, 2 * gw), F32),
        pltpu.VMEM((N_SLAB, CONV_HALO + ts, LANE), F32),
        pltpu.VMEM((N_SLAB, ts, LANE), F32),
        pltpu.VMEM((N_SLAB, LRU_HALO + ts, LANE), F32),
        pltpu.VMEM((ts, d), F32),
        pltpu.VMEM((ts, d), BF16),
        pltpu.VMEM((ts, d), F32),
        pltpu.VMEM((ts, d), F32),
        pltpu.VMEM((SUBLANE, d), F32),
        pltpu.VMEM((ts, 2 * d), BF16),
    ]
    return pl.pallas_call(
        _even_kernel,
        grid=(bsz, seq // ts),
        in_specs=[
            tile,
            pl.BlockSpec((1, 3, d), lambda b, s: (b, 0, 0)),
            _const_spec((1, d)), _const_spec((1, d)),
            _const_spec(w_vg.shape), _const_spec(w_rest.shape),
            _const_spec(cw.shape), _const_spec(cb.shape),
            _const_spec((1, d)), _const_spec((1, d)),
            _const_spec(lcw.shape), _const_spec((1, d)),
            _const_spec(wg.shape),
            _const_spec((1, d)), _const_spec((1, d)), _const_spec((1, d)),
            _const_spec(w_out.shape),
        ],
        out_specs=tile,
        out_shape=jax.ShapeDtypeStruct(x.shape, F32),
        scratch_shapes=scratch,
        compiler_params=_params(2),
        name="even_layer",
    )(x, mod, row(pre_g), row(post_g), w_vg, w_rest, cw, cb, row(ln_g), row(ln_b),
      lcw, row(lcb), wg, row(ba), row(bx), row(lam), w_out.astype(BF16))


def _mla_proj_kernel(x_ref, mod_ref, pre_g_ref, w_c_ref, w_z_ref, qn_ref, kvn_ref,
                     wuq_t_ref, wk_ref, wv_t_ref, cn_ref, sn_ref, ct_ref, st_ref,
                     qt_ref, k_ref, vt_ref, sz_ref, hbf):
    ts = x_ref.shape[1]
    half = QK_ROPE // 2
    hbf[...] = _prenorm(x_ref[0], mod_ref, pre_g_ref).astype(BF16)

    uc = _dot(hbf[...], w_c_ref[...])
    sz_ref[0] = _silu(_dot(hbf[...], w_z_ref[...])).astype(BF16)

    def rms(v, g_ref):
        ms = jnp.mean(v * v, axis=-1, keepdims=True)
        return (v * lax.rsqrt(ms + EPS) * g_ref[...]).astype(BF16)

    cqn = rms(uc[:, 0:Q_LORA], qn_ref)
    ckvn = rms(uc[:, Q_LORA:Q_LORA + KV_LORA], kvn_ref)
    kr = uc[:, Q_LORA + KV_LORA:Q_LORA + KV_LORA + QK_ROPE]

    q_t = _dot_nt(wuq_t_ref[...], cqn)
    c_t = ct_ref[0]
    s_t = st_ref[0]
    scale = (QK_NOPE + QK_ROPE) ** -0.5 * LOG2_E
    for hd in range(MLA_HEADS):
        base = hd * QK_PAD
        x1 = q_t[base + QK_NOPE:base + QK_NOPE + half]
        x2 = q_t[base + QK_NOPE + half:base + QK_NOPE + QK_ROPE]
        blk = jnp.concatenate([
            q_t[base:base + QK_NOPE],
            x1 * c_t - x2 * s_t,
            x1 * s_t + x2 * c_t,
            q_t[base + QK_NOPE + QK_ROPE:base + QK_PAD],
        ], axis=0)
        qt_ref[0, hd, 0] = (blk * scale).astype(BF16)

    kv_k = _dot(ckvn, wk_ref[...])
    c_n = cn_ref[0]
    s_n = sn_ref[0]
    k1 = kr[:, 0:half]
    k2 = kr[:, half:QK_ROPE]
    kr128 = jnp.concatenate([
        jnp.zeros((ts, QK_NOPE), F32),
        k1 * c_n - k2 * s_n,
        k1 * s_n + k2 * c_n,
        jnp.zeros((ts, QK_PAD - QK_NOPE - QK_ROPE), F32),
    ], axis=-1)
    for hd in range(MLA_HEADS):
        k_ref[0, hd] = (kv_k[:, hd * QK_PAD:(hd + 1) * QK_PAD] + kr128).astype(BF16)

    v_t = _dot_nt(wv_t_ref[...], ckvn)
    for hd in range(MLA_HEADS):
        vt_ref[0, hd, 0:V_HEAD, :] = v_t[hd * V_HEAD:(hd + 1) * V_HEAD].astype(BF16)
        row = lax.broadcasted_iota(jnp.int32, (V_AUG - V_HEAD, ts), 0)
        vt_ref[0, hd, V_HEAD:V_AUG, :] = jnp.where(row == 0, 1.0, 0.0).astype(BF16)


def _mla_proj_call(x, mod, pre_g, w_in, q_norm, kv_norm, w_uq, w_ukv, cn, sn, ct, st):
    bsz, seq, d = x.shape
    ts = SEQ_TILE
    nt = seq // ts
    half = QK_ROPE // 2
    n_c = Q_LORA + KV_LORA + QK_ROPE
    w_c = jnp.pad(w_in[:, :n_c], ((0, 0), (0, (-n_c) % LANE))).astype(BF16)
    w_z = w_in[:, n_c:].astype(BF16)
    wq = w_uq.reshape(Q_LORA, MLA_HEADS, QK_NOPE + QK_ROPE)
    wq = jnp.pad(wq, ((0, 0), (0, 0), (0, QK_PAD - QK_NOPE - QK_ROPE)))
    wuq_t = wq.reshape(Q_LORA, MLA_HEADS * QK_PAD).T.astype(BF16)
    wkv = w_ukv.reshape(KV_LORA, MLA_HEADS, QK_NOPE + V_HEAD)
    wk = jnp.pad(wkv[:, :, :QK_NOPE], ((0, 0), (0, 0), (0, QK_PAD - QK_NOPE)))
    wk = wk.reshape(KV_LORA, MLA_HEADS * QK_PAD).astype(BF16)
    wv_t = wkv[:, :, QK_NOPE:].reshape(KV_LORA, MLA_HEADS * V_HEAD).T.astype(BF16)

    tile = pl.BlockSpec((1, ts, d), lambda b, s: (b, s, 0))
    return pl.pallas_call(
        _mla_proj_kernel,
        grid=(bsz, nt),
        in_specs=[
            tile,
            pl.BlockSpec((1, 3, d), lambda b, s: (b, 0, 0)),
            _const_spec((1, d)),
            _const_spec(w_c.shape), _const_spec(w_z.shape),
            _const_spec((1, Q_LORA)), _const_spec((1, KV_LORA)),
            _const_spec(wuq_t.shape), _const_spec(wk.shape), _const_spec(wv_t.shape),
            pl.BlockSpec((1, ts, half), lambda b, s: (b, s, 0)),
            pl.BlockSpec((1, ts, half), lambda b, s: (b, s, 0)),
            pl.BlockSpec((1, half, ts), lambda b, s: (b, 0, s)),
            pl.BlockSpec((1, half, ts), lambda b, s: (b, 0, s)),
        ],
        out_specs=[
            pl.BlockSpec((1, MLA_HEADS, 1, QK_PAD, ts), lambda b, s: (b, 0, s, 0, 0)),
            pl.BlockSpec((1, MLA_HEADS, ts, QK_PAD), lambda b, s: (b, 0, s, 0)),
            pl.BlockSpec((1, MLA_HEADS, V_AUG, ts), lambda b, s: (b, 0, 0, s)),
            tile,
        ],
        out_shape=[
            jax.ShapeDtypeStruct((bsz, MLA_HEADS, nt, QK_PAD, ts), BF16),
            jax.ShapeDtypeStruct((bsz, MLA_HEADS, seq, QK_PAD), BF16),
            jax.ShapeDtypeStruct((bsz, MLA_HEADS, V_AUG, seq), BF16),
            jax.ShapeDtypeStruct((bsz, seq, d), BF16),
        ],
        scratch_shapes=[pltpu.VMEM((ts, d), BF16)],
        compiler_params=_params(2),
        name="mla_proj",
    )(x, mod, pre_g.reshape(1, d), w_c, w_z, q_norm.reshape(1, Q_LORA), kv_norm.reshape(1, KV_LORA),
      wuq_t, wk, wv_t, cn, sn, ct, st)


def _mla_attn_kernel(qt_ref, k_ref, vt_ref, sz_ref, o_ref, s_even, s_odd, m_even, m_odd, p_scr):
    tq = qt_ref.shape[-1]
    nt = k_ref.shape[2] // tq
    t = pl.program_id(2)
    heads = range(HEAD_GROUP)

    def step(v):
        n_a = v + 1 if v < nt else 0
        n_b = v if v > 0 else 0
        s_a, m_a = (s_even, m_even) if v % 2 == 0 else (s_odd, m_odd)
        s_b, m_bref = (s_odd, m_odd) if v % 2 == 0 else (s_even, m_even)
        key = lax.broadcasted_iota(jnp.int32, (tq, tq), 0)
        qry = lax.broadcasted_iota(jnp.int32, (tq, tq), 1)
        q_t = [qt_ref[0, hh, 0] for hh in heads] if n_a else None
        m_b = [m_bref[hh] for hh in heads] if n_b else None
        m_part = [None] * HEAD_GROUP
        for j in range(max(n_a, n_b)):
            blk = slice(j * tq, (j + 1) * tq)
            if j < n_a:
                for hh in heads:
                    s = _dot(k_ref[0, hh, blk, :], q_t[hh])
                    if j == v:
                        s = jnp.where(key <= qry, s, NEG_BIG)
                    s_a[hh, blk, :] = s
                    pm = jnp.max(s.reshape(tq // MAX_PART, MAX_PART, tq), axis=0)
                    m_part[hh] = pm if m_part[hh] is None else jnp.maximum(m_part[hh], pm)
            if j < n_b:
                for hh in heads:
                    p = jnp.exp2(s_b[hh, blk, :] - m_b[hh]).astype(BF16)
                    p_scr[hh, blk, :] = p
            if n_b and j == n_b - 1:
                kend = n_b * tq
                outs = []
                for hh in heads:
                    o_aug = _dot(vt_ref[0, hh, :, 0:kend], p_scr[hh, 0:kend, :])
                    outs.append(o_aug[0:V_HEAD] / o_aug[V_HEAD:V_HEAD + 1])
                o = jnp.concatenate(outs, axis=0).T
                o_ref[0] = (o * sz_ref[0].astype(F32)).astype(BF16)
        if n_a:
            for hh in heads:
                m_a[hh] = jnp.max(m_part[hh], axis=0, keepdims=True)

    for v in range(nt + 1):
        pl.when(t == v)(functools.partial(step, v))


def _mla_attn_call(qt, k, vt, sz):
    bsz, heads, nt, _, ts = qt.shape
    seq = nt * ts
    wide = HEAD_GROUP * V_HEAD
    prev = lambda t: jnp.maximum(t - 1, 0)
    return pl.pallas_call(
        _mla_attn_kernel,
        grid=(bsz, heads // HEAD_GROUP, nt + 1),
        in_specs=[
            pl.BlockSpec((1, HEAD_GROUP, 1, QK_PAD, ts),
                         lambda b, p, t: (b, p, jnp.minimum(t, nt - 1), 0, 0)),
            pl.BlockSpec((1, HEAD_GROUP, seq, QK_PAD), lambda b, p, t: (b, p, 0, 0)),
            pl.BlockSpec((1, HEAD_GROUP, V_AUG, seq), lambda b, p, t: (b, p, 0, 0)),
            pl.BlockSpec((1, ts, wide), lambda b, p, t: (b, prev(t), p)),
        ],
        out_specs=pl.BlockSpec((1, ts, wide), lambda b, p, t: (b, prev(t), p)),
        out_shape=jax.ShapeDtypeStruct((bsz, seq, heads * V_HEAD), BF16),
        scratch_shapes=[pltpu.VMEM((HEAD_GROUP, seq, ts), F32),
                        pltpu.VMEM((HEAD_GROUP, seq, ts), F32),
                        pltpu.VMEM((HEAD_GROUP, 1, ts), F32),
                        pltpu.VMEM((HEAD_GROUP, 1, ts), F32),
                        pltpu.VMEM((HEAD_GROUP, seq, ts), BF16)],
        compiler_params=_params(3),
        name="mla_attn",
    )(qt, k, vt, sz)


def _mla_out_kernel(x_ref, o_ref_in, mod_ref, post_g_ref, w_out_ref, out_ref):
    y = _dot(o_ref_in[0], w_out_ref[...])
    out_ref[0] = _postnorm_residual(x_ref[0], y, mod_ref, post_g_ref)


def _mla_out_call(x, o, mod, post_g, w_out):
    bsz, seq, d = x.shape
    ts = SEQ_TILE
    tile = pl.BlockSpec((1, ts, d), lambda b, s: (b, s, 0))
    return pl.pallas_call(
        _mla_out_kernel,
        grid=(bsz, seq // ts),
        in_specs=[
            tile, tile,
            pl.BlockSpec((1, 3, d), lambda b, s: (b, 0, 0)),
            _const_spec((1, d)),
            _const_spec(w_out.shape),
        ],
        out_specs=tile,
        out_shape=jax.ShapeDtypeStruct(x.shape, F32),
        compiler_params=_params(2),
        name="mla_out",
    )(x, o, mod, post_g.reshape(1, d), w_out.astype(BF16))


def kernel(x, c, positions, ada_w, ada_b, pre_g, post_g, ev_w_in, ev_conv_w, ev_conv_b, ev_ln_g, ev_ln_b, ev_lru_conv_w, ev_lru_conv_b, ev_lru_wa, ev_lru_ba, ev_lru_wx, ev_lru_bx, ev_lru_lam, ev_w_out, od_w_in, od_q_norm, od_kv_norm, od_w_uq, od_w_ukv, od_w_out):
    depth = ada_w.shape[0]
    bsz, seq, d = x.shape
    assert d == D_MODEL and seq % SEQ_TILE == 0
    mod = _ada_call(c, ada_w, ada_b).reshape(depth, bsz, 3, d)
    cn, sn, ct, st = _rope_call(positions)
    for layer in range(depth):
        j = layer // 2
        if layer % 2 == 0:
            x = _even_call(x, mod[layer], pre_g[layer], post_g[layer], ev_w_in[j], ev_conv_w[j],
                           ev_conv_b[j], ev_ln_g[j], ev_ln_b[j], ev_lru_conv_w[j], ev_lru_conv_b[j],
                           ev_lru_wa[j], ev_lru_ba[j], ev_lru_wx[j], ev_lru_bx[j], ev_lru_lam[j],
                           ev_w_out[j])
        else:
            qt, k, vt, sz = _mla_proj_call(x, mod[layer], pre_g[layer], od_w_in[j], od_q_norm[j],
                                           od_kv_norm[j], od_w_uq[j], od_w_ukv[j], cn, sn, ct, st)
            o = _mla_attn_call(qt, k, vt, sz)
            x = _mla_out_call(x, o, mod[layer], post_g[layer], od_w_out[j])
    return x
```

```python
import functools

import jax
import jax.numpy as jnp
from jax import lax
from jax.experimental import pallas as pl
from jax.experimental.pallas import tpu as pltpu

F32 = jnp.float32
BF16 = jnp.bfloat16

D_MODEL = 1024
CONV_KERNEL = 31
LRU_CONV_KERNEL = 4
LRU_HEADS = 16
LRU_HEAD_DIM = D_MODEL // LRU_HEADS
LRU_C = 8.0
MLA_HEADS = 16
QK_NOPE = 64
QK_ROPE = 32
V_HEAD = 64
Q_LORA = 256
KV_LORA = 256
ROPE_THETA = 10000.0
EPS = 1e-6

LANE = 128
SUBLANE = 8
MXU_DIM = 256

SEQ_TILE = 256
N_SLAB = D_MODEL // LANE
CONV_HALO = 32
LRU_HALO = 8
CONV_TAPS_PAD = 32
GATE_GROUP = MXU_DIM
N_GATE_GROUP = D_MODEL // GATE_GROUP
CONV_ROWS = 64
QK_PAD = LANE
HEAD_GROUP = 4
MAX_PART = 32
V_AUG = V_HEAD + 16
LOG2_E = 1.4426950408889634
NEG_BIG = -1e30

VMEM_LIMIT_BYTES = 56 * 1024 * 1024


def _dot(a, b):
    return jnp.dot(a, b, preferred_element_type=F32)


def _dot_nt(a, b):
    return lax.dot_general(a, b, (((1,), (1,)), ((), ())), preferred_element_type=F32)


def _sigmoid(x):
    return 0.5 * jnp.tanh(0.5 * x) + 0.5


def _silu(x):
    h = 0.5 * x
    return h + h * jnp.tanh(h)


def _const_spec(shape):
    nd = len(shape)
    return pl.BlockSpec(shape, lambda *_: (0,) * nd, pipeline_mode=pl.Buffered(1))


def _params(n_grid):
    return pltpu.CompilerParams(
        dimension_semantics=("arbitrary",) * n_grid, vmem_limit_bytes=VMEM_LIMIT_BYTES)


def _ada_kernel(c_ref, w_ref, b_ref, o_ref):
    c = c_ref[...]
    o_ref[0] = jnp.dot(_silu(c), w_ref[0], preferred_element_type=F32,
                       precision=lax.Precision.HIGHEST) + b_ref[0]


def _ada_call(c, ada_w, ada_b):
    depth, d, d3 = ada_w.shape
    bsz = c.shape[0]
    n_col = d3 // d
    return pl.pallas_call(
        _ada_kernel,
        grid=(depth, n_col),
        in_specs=[
            pl.BlockSpec((bsz, d), lambda l, j: (0, 0)),
            pl.BlockSpec((1, d, d), lambda l, j: (l, 0, j)),
            pl.BlockSpec((1, 1, d), lambda l, j: (l, 0, j)),
        ],
        out_specs=pl.BlockSpec((1, bsz, d), lambda l, j: (l, 0, j)),
        out_shape=jax.ShapeDtypeStruct((depth, bsz, d3), F32),
        compiler_params=_params(2),
        name="ada_mod",
    )(c, ada_w, ada_b.reshape(depth, 1, d3))


def _rope_kernel(posrep_ref, invrep_ref, pos_ref, invcol_ref, cn_ref, sn_ref, ct_ref, st_ref):
    ang_n = posrep_ref[0].astype(F32) * invrep_ref[...]
    cn_ref[0] = jnp.cos(ang_n)
    sn_ref[0] = jnp.sin(ang_n)
    ang_t = invcol_ref[...] * pos_ref[0].astype(F32)
    ct_ref[0] = jnp.cos(ang_t)
    st_ref[0] = jnp.sin(ang_t)


def _rope_call(positions):
    bsz, seq = positions.shape
    half = QK_ROPE // 2
    rep = LANE // half
    inv = ROPE_THETA ** (-jnp.arange(0, QK_ROPE, 2, dtype=F32) / QK_ROPE)
    posrep = jnp.repeat(positions, half, axis=1).reshape(bsz, seq // rep, LANE)
    invrep = jnp.tile(inv, rep).reshape(1, LANE)
    n_spec = pl.BlockSpec((1, seq // rep, LANE), lambda b: (b, 0, 0))
    t_spec = pl.BlockSpec((1, half, seq), lambda b: (b, 0, 0))
    cn, sn, ct, st = pl.pallas_call(
        _rope_kernel,
        grid=(bsz,),
        in_specs=[
            n_spec,
            pl.BlockSpec((1, LANE), lambda b: (0, 0)),
            pl.BlockSpec((1, 1, seq), lambda b: (b, 0, 0)),
            pl.BlockSpec((half, 1), lambda b: (0, 0)),
        ],
        out_specs=[n_spec, n_spec, t_spec, t_spec],
        out_shape=[jax.ShapeDtypeStruct((bsz, seq // rep, LANE), F32)] * 2
        + [jax.ShapeDtypeStruct((bsz, half, seq), F32)] * 2,
        compiler_params=_params(1),
        name="rope_tables",
    )(posrep, invrep, positions.reshape(bsz, 1, seq), inv.reshape(half, 1))
    return cn.reshape(bsz, seq, half), sn.reshape(bsz, seq, half), ct, st


def _prenorm(x, mod_ref, pre_g_ref):
    ms = jnp.mean(x * x, axis=-1, keepdims=True)
    shift = mod_ref[0, 0:1, :]
    scale = mod_ref[0, 1:2, :]
    gmul = pre_g_ref[...] * (1.0 + scale)
    return x * lax.rsqrt(ms + EPS) * gmul + shift


def _postnorm_residual(x, y, mod_ref, post_g_ref):
    ms = jnp.mean(y * y, axis=-1, keepdims=True)
    gate = mod_ref[0, 2:3, :]
    return x + (gate * post_g_ref[...]) * (y * lax.rsqrt(ms + EPS))


def _scan_slab(a, b, carry):
    t = a.shape[0]
    groups = t // SUBLANE
    a = a.reshape(groups, SUBLANE, LANE)
    b = b.reshape(groups, SUBLANE, LANE)
    row = lax.broadcasted_iota(jnp.int32, a.shape, 1)
    d = 1
    while d < SUBLANE:
        ok = row >= d
        b = jnp.where(ok, a, 0.0) * pltpu.roll(b, d, axis=1) + b
        a = a * jnp.where(ok, pltpu.roll(a, d, axis=1), 1.0)
        d *= 2
    outs = []
    c = carry
    for v in range(groups):
        hv = b[v] + a[v] * c
        outs.append(hv)
        c = jnp.broadcast_to(hv[SUBLANE - 1:SUBLANE, :], (SUBLANE, LANE))
    return jnp.concatenate(outs, axis=0), c


def _even_kernel(x_ref, mod_ref, pre_g_ref, post_g_ref, w_vg_ref, w_rest_ref, cw_ref, cb_ref, lng_ref,
                 lnb_ref, lcw_ref, lcb_ref, wg_ref, ba_ref, bx_ref, lam_ref, w_out_ref, o_ref,
                 hbf, u0, u1, u2, ur, cbuf, cv, lbuf, xcf, xcb, a_s, b_s, hcar, act):
    ts = x_ref.shape[1]
    d = D_MODEL
    gw = GATE_GROUP
    spg = gw // LANE

    def rest(kind, c):
        lo = kind * gw + (c % spg) * LANE
        return ur[c // spg, :, lo:lo + LANE]

    @pl.when(pl.program_id(1) == 0)
    def _():
        cbuf[:, 0:CONV_HALO, :] = jnp.zeros((N_SLAB, CONV_HALO, LANE), F32)
        lbuf[:, 0:LRU_HALO, :] = jnp.zeros((N_SLAB, LRU_HALO, LANE), F32)
        hcar[...] = jnp.zeros(hcar.shape, F32)

    hbf[...] = _prenorm(x_ref[0], mod_ref, pre_g_ref).astype(BF16)

    u0[...] = _dot(hbf[...], w_vg_ref[:, 0:d])
    u1[...] = _dot(hbf[...], w_vg_ref[:, d:2 * d])
    u2[...] = _dot(hbf[...], w_vg_ref[:, 2 * d:3 * d])
    for g in range(N_GATE_GROUP):
        ur[g] = _dot(hbf[...], w_rest_ref[g])
    for c in range(N_SLAB):
        sl = slice(c * LANE, (c + 1) * LANE)
        cbuf[c, CONV_HALO:CONV_HALO + ts, :] = u0[:, sl] * _sigmoid(u1[:, sl])

    first = CONV_HALO - (CONV_KERNEL - 1)
    tap_groups = [[j for j in range(CONV_KERNEL) if (first + j) % SUBLANE == r] for r in range(SUBLANE)]

    def conv_slab(c, carry):
        w = cw_ref[c]
        for r0 in range(0, ts, CONV_ROWS):
            acc = jnp.broadcast_to(cb_ref[c], (CONV_ROWS, LANE))
            for taps in tap_groups:
                start = first + taps[0] + r0
                win = cbuf[c, start:start + CONV_ROWS + SUBLANE * (len(taps) - 1), :]
                for q, j in enumerate(taps):
                    acc = acc + win[SUBLANE * q:SUBLANE * q + CONV_ROWS] * w[j:j + 1, :]
            cv[c, r0:r0 + CONV_ROWS, :] = acc
        return carry

    lax.fori_loop(0, N_SLAB, conv_slab, 0)

    tot = cv[0]
    for c in range(1, N_SLAB):
        tot = tot + cv[c]
    mu = jnp.broadcast_to(jnp.sum(tot, axis=-1, keepdims=True) * (1.0 / d), (ts, LANE))
    sq = jnp.zeros((ts, LANE), F32)
    for c in range(N_SLAB):
        dc = cv[c] - mu
        sq = sq + dc * dc
    rstd = lax.rsqrt(jnp.sum(sq, axis=-1, keepdims=True) * (1.0 / d) + EPS)
    rstd = jnp.broadcast_to(rstd, (ts, LANE))
    for c in range(N_SLAB):
        sl = slice(c * LANE, (c + 1) * LANE)
        yn = (cv[c] - mu) * rstd * lng_ref[:, sl] + lnb_ref[:, sl]
        act[:, sl] = (_silu(yn) * _silu(rest(0, c))).astype(BF16)

    for c in range(N_SLAB):
        sl = slice(c * LANE, (c + 1) * LANE)
        lbuf[c, LRU_HALO:LRU_HALO + ts, :] = u2[:, sl]
        acc = jnp.broadcast_to(lcb_ref[:, sl], (ts, LANE))
        for j in range(LRU_CONV_KERNEL):
            start = LRU_HALO - (LRU_CONV_KERNEL - 1) + j
            acc = acc + lbuf[c, start:start + ts, :] * lcw_ref[j:j + 1, sl]
        xcf[:, sl] = acc
        xcb[:, sl] = acc.astype(BF16)

    nl = -lam_ref[...]
    softplus = jnp.maximum(nl, 0.0) + jnp.log1p(jnp.exp(-jnp.abs(nl)))
    for g in range(N_GATE_GROUP):
        sl = slice(g * gw, (g + 1) * gw)
        rg = _dot(xcb[:, sl], wg_ref[g])
        r = _sigmoid(rg[:, 0:gw] + ba_ref[:, sl])
        i = _sigmoid(rg[:, gw:2 * gw] + bx_ref[:, sl])
        a = jnp.exp((-LRU_C) * r * softplus[:, sl])
        a_s[:, sl] = a
        b_s[:, sl] = jnp.sqrt(1.0 - a * a) * i * xcf[:, sl]

    for c in range(N_SLAB):
        sl = slice(c * LANE, (c + 1) * LANE)
        h, c_new = _scan_slab(a_s[:, sl], b_s[:, sl], hcar[:, sl])
        hcar[:, sl] = c_new
        act[:, d + c * LANE:d + (c + 1) * LANE] = (h * _silu(rest(1, c))).astype(BF16)

    cbuf[:, 0:CONV_HALO, :] = cbuf[:, ts:ts + CONV_HALO, :]
    lbuf[:, 0:LRU_HALO, :] = lbuf[:, ts:ts + LRU_HALO, :]

    y = _dot(act[...], w_out_ref[...])
    o_ref[0] = _postnorm_residual(x_ref[0], y, mod_ref, post_g_ref)


def _even_call(x, mod, pre_g, post_g, w_in, conv_w, conv_b, ln_g, ln_b, lcw, lcb,
               wa, ba, wx, bx, lam, w_out):
    bsz, seq, d = x.shape
    ts = SEQ_TILE
    row = lambda v: v.reshape(1, d)
    cw = jnp.pad(conv_w, ((0, CONV_TAPS_PAD - CONV_KERNEL), (0, 0)))
    cw = cw.reshape(CONV_TAPS_PAD, N_SLAB, LANE).transpose(1, 0, 2)
    cb = conv_b.reshape(N_SLAB, 1, LANE)
    hpg = GATE_GROUP // LRU_HEAD_DIM
    eye = jnp.eye(hpg, dtype=F32)

    def blockdiag(w):
        wg = w.reshape(N_GATE_GROUP, hpg, LRU_HEAD_DIM, LRU_HEAD_DIM)
        return jnp.einsum('ghij,hk->ghikj', wg, eye).reshape(N_GATE_GROUP, GATE_GROUP, GATE_GROUP)

    wg = jnp.concatenate([blockdiag(wa), blockdiag(wx)], axis=-1).astype(BF16)
    ng, gw = N_GATE_GROUP, GATE_GROUP
    w_vg = jnp.concatenate([w_in[:, 0:2 * d], w_in[:, 3 * d:4 * d]], axis=1).astype(BF16)
    grouped = lambda lo: w_in[:, lo * d:(lo + 1) * d].reshape(d, ng, gw)
    w_rest = jnp.concatenate([grouped(2), grouped(4)], axis=2)
    w_rest = w_rest.transpose(1, 0, 2).astype(BF16)

    tile = pl.BlockSpec((1, ts, d), lambda b, s: (b, s, 0))
    scratch = [
        pltpu.VMEM((ts, d), BF16),
        pltpu.VMEM((ts, d), F32),
        pltpu.VMEM((ts, d), F32),
        pltpu.VMEM((ts, d), F32),
        pltpu.VMEM((ng, ts, 2 * gw), F32),
        pltpu.VMEM((N_SLAB, CONV_HALO + ts, LANE), F32),
        pltpu.VMEM((N_SLAB, ts, LANE), F32),
        pltpu.VMEM((N_SLAB, LRU_HALO + ts, LANE), F32),
        pltpu.VMEM((ts, d), F32),
        pltpu.VMEM((ts, d), BF16),
        pltpu.VMEM((ts, d), F32),
        pltpu.VMEM((ts, d), F32),
        pltpu.VMEM((SUBLANE, d), F32),
        pltpu.VMEM((ts, 2 * d), BF16),
    ]
    return pl.pallas_call(
        _even_kernel,
        grid=(bsz, seq // ts),
        in_specs=[
            tile,
            pl.BlockSpec((1, 3, d), lambda b, s: (b, 0, 0)),
            _const_spec((1, d)), _const_spec((1, d)),
            _const_spec(w_vg.shape), _const_spec(w_rest.shape),
            _const_spec(cw.shape), _const_spec(cb.shape),
            _const_spec((1, d)), _const_spec((1, d)),
            _const_spec(lcw.shape), _const_spec((1, d)),
            _const_spec(wg.shape),
            _const_spec((1, d)), _const_spec((1, d)), _const_spec((1, d)),
            _const_spec(w_out.shape),
        ],
        out_specs=tile,
        out_shape=jax.ShapeDtypeStruct(x.shape, F32),
        scratch_shapes=scratch,
        compiler_params=_params(2),
        name="even_layer",
    )(x, mod, row(pre_g), row(post_g), w_vg, w_rest, cw, cb, row(ln_g), row(ln_b),
      lcw, row(lcb), wg, row(ba), row(bx), row(lam), w_out.astype(BF16))


def _mla_proj_kernel(x_ref, mod_ref, pre_g_ref, w_c_ref, w_z_ref, qn_ref, kvn_ref,
                     wuq_t_ref, wk_ref, wv_t_ref, cn_ref, sn_ref, ct_ref, st_ref,
                     qt_ref, k_ref, vt_ref, sz_ref, hbf):
    ts = x_ref.shape[1]
    half = QK_ROPE // 2
    hbf[...] = _prenorm(x_ref[0], mod_ref, pre_g_ref).astype(BF16)

    uc = _dot(hbf[...], w_c_ref[...])
    sz_ref[0, 0, 0] = _silu(_dot(hbf[...], w_z_ref[...])).astype(BF16)

    def rms(v, g_ref):
        ms = jnp.mean(v * v, axis=-1, keepdims=True)
        return (v * lax.rsqrt(ms + EPS) * g_ref[...]).astype(BF16)

    cqn = rms(uc[:, 0:Q_LORA], qn_ref)
    ckvn = rms(uc[:, Q_LORA:Q_LORA + KV_LORA], kvn_ref)
    kr = uc[:, Q_LORA + KV_LORA:Q_LORA + KV_LORA + QK_ROPE]

    q_t = _dot_nt(wuq_t_ref[...], cqn)
    c_t = ct_ref[0]
    s_t = st_ref[0]
    scale = (QK_NOPE + QK_ROPE) ** -0.5 * LOG2_E
    for hd in range(MLA_HEADS):
        base = hd * QK_PAD
        x1 = q_t[base + QK_NOPE:base + QK_NOPE + half]
        x2 = q_t[base + QK_NOPE + half:base + QK_NOPE + QK_ROPE]
        blk = jnp.concatenate([
            q_t[base:base + QK_NOPE],
            x1 * c_t - x2 * s_t,
            x1 * s_t + x2 * c_t,
            q_t[base + QK_NOPE + QK_ROPE:base + QK_PAD],
        ], axis=0)
        qt_ref[0, hd, 0, 0] = (blk * scale).astype(BF16)

    kv_k = _dot(ckvn, wk_ref[...])
    c_n = cn_ref[0]
    s_n = sn_ref[0]
    k1 = kr[:, 0:half]
    k2 = kr[:, half:QK_ROPE]
    kr128 = jnp.concatenate([
        jnp.zeros((ts, QK_NOPE), F32),
        k1 * c_n - k2 * s_n,
        k1 * s_n + k2 * c_n,
        jnp.zeros((ts, QK_PAD - QK_NOPE - QK_ROPE), F32),
    ], axis=-1)
    for hd in range(MLA_HEADS):
        k_ref[0, hd] = (kv_k[:, hd * QK_PAD:(hd + 1) * QK_PAD] + kr128).astype(BF16)

    v_t = _dot_nt(wv_t_ref[...], ckvn)
    for hd in range(MLA_HEADS):
        vt_ref[0, hd, 0:V_HEAD, :] = v_t[hd * V_HEAD:(hd + 1) * V_HEAD].astype(BF16)
        row = lax.broadcasted_iota(jnp.int32, (V_AUG - V_HEAD, ts), 0)
        vt_ref[0, hd, V_HEAD:V_AUG, :] = jnp.where(row == 0, 1.0, 0.0).astype(BF16)


def _mla_proj_call(x, mod, pre_g, w_in, q_norm, kv_norm, w_uq, w_ukv, cn, sn, ct, st):
    bsz, seq, d = x.shape
    ts = SEQ_TILE
    nt = seq // ts
    half = QK_ROPE // 2
    n_c = Q_LORA + KV_LORA + QK_ROPE
    w_c = jnp.pad(w_in[:, :n_c], ((0, 0), (0, (-n_c) % LANE))).astype(BF16)
    w_z = w_in[:, n_c:].astype(BF16)
    wq = w_uq.reshape(Q_LORA, MLA_HEADS, QK_NOPE + QK_ROPE)
    wq = jnp.pad(wq, ((0, 0), (0, 0), (0, QK_PAD - QK_NOPE - QK_ROPE)))
    wuq_t = wq.reshape(Q_LORA, MLA_HEADS * QK_PAD).T.astype(BF16)
    wkv = w_ukv.reshape(KV_LORA, MLA_HEADS, QK_NOPE + V_HEAD)
    wk = jnp.pad(wkv[:, :, :QK_NOPE], ((0, 0), (0, 0), (0, QK_PAD - QK_NOPE)))
    wk = wk.reshape(KV_LORA, MLA_HEADS * QK_PAD).astype(BF16)
    wv_t = wkv[:, :, QK_NOPE:].reshape(KV_LORA, MLA_HEADS * V_HEAD).T.astype(BF16)

    tile = pl.BlockSpec((1, ts, d), lambda b, s: (b, s, 0))
    return pl.pallas_call(
        _mla_proj_kernel,
        grid=(bsz, nt),
        in_specs=[
            tile,
            pl.BlockSpec((1, 3, d), lambda b, s: (b, 0, 0)),
            _const_spec((1, d)),
            _const_spec(w_c.shape), _const_spec(w_z.shape),
            _const_spec((1, Q_LORA)), _const_spec((1, KV_LORA)),
            _const_spec(wuq_t.shape), _const_spec(wk.shape), _const_spec(wv_t.shape),
            pl.BlockSpec((1, ts, half), lambda b, s: (b, s, 0)),
            pl.BlockSpec((1, ts, half), lambda b, s: (b, s, 0)),
            pl.BlockSpec((1, half, ts), lambda b, s: (b, 0, s)),
            pl.BlockSpec((1, half, ts), lambda b, s: (b, 0, s)),
        ],
        out_specs=[
            pl.BlockSpec((1, MLA_HEADS, 1, 1, QK_PAD, ts),
                         lambda b, s: (b, 0, *_pair_index(s, nt), 0, 0)),
            pl.BlockSpec((1, MLA_HEADS, ts, QK_PAD), lambda b, s: (b, 0, s, 0)),
            pl.BlockSpec((1, MLA_HEADS, V_AUG, ts), lambda b, s: (b, 0, 0, s)),
            pl.BlockSpec((1, 1, 1, ts, d), lambda b, s: (b, *_pair_index(s, nt), 0, 0)),
        ],
        out_shape=[
            jax.ShapeDtypeStruct((bsz, MLA_HEADS, nt // 2, 2, QK_PAD, ts), BF16),
            jax.ShapeDtypeStruct((bsz, MLA_HEADS, seq, QK_PAD), BF16),
            jax.ShapeDtypeStruct((bsz, MLA_HEADS, V_AUG, seq), BF16),
            jax.ShapeDtypeStruct((bsz, nt // 2, 2, ts, d), BF16),
        ],
        scratch_shapes=[pltpu.VMEM((ts, d), BF16)],
        compiler_params=_params(2),
        name="mla_proj",
    )(x, mod, pre_g.reshape(1, d), w_c, w_z, q_norm.reshape(1, Q_LORA), kv_norm.reshape(1, KV_LORA),
      wuq_t, wk, wv_t, cn, sn, ct, st)


def _mla_attn_kernel(qt_ref, k_ref, vt_ref, sz_ref, o_ref,
                     s_lo0, s_hi0, s_lo1, s_hi1, m_lo0, m_hi0, m_lo1, m_hi1, p_lo, p_hi):
    tq = qt_ref.shape[-1]
    nt = k_ref.shape[2] // tq
    n_pair = nt // 2
    u_id = pl.program_id(2)
    heads = range(HEAD_GROUP)
    s_bufs = (((s_lo0, m_lo0), (s_hi0, m_hi0)), ((s_lo1, m_lo1), (s_hi1, m_hi1)))
    p_bufs = (p_lo, p_hi)

    def streams(slot, v_a, v_b, bufs_a, bufs_b):
        n_a = v_a + 1 if v_a is not None else 0
        n_b = v_b + 1 if v_b is not None else 0
        s_a, m_a = bufs_a
        s_b, m_bref = bufs_b
        p_scr = p_bufs[slot]
        key = lax.broadcasted_iota(jnp.int32, (tq, tq), 0)
        qry = lax.broadcasted_iota(jnp.int32, (tq, tq), 1)
        q_t = [qt_ref[0, hh, 0, slot] for hh in heads] if n_a else None
        m_b = [m_bref[hh] for hh in heads] if n_b else None
        m_part = [None] * HEAD_GROUP
        for j in range(max(n_a, n_b)):
            blk = slice(j * tq, (j + 1) * tq)
            if j < n_a:
                for hh in heads:
                    s = _dot(k_ref[0, hh, blk, :], q_t[hh])
                    if j == v_a:
                        s = jnp.where(key <= qry, s, NEG_BIG)
                    s_a[hh, blk, :] = s
                    pm = jnp.max(s.reshape(tq // MAX_PART, MAX_PART, tq), axis=0)
                    m_part[hh] = pm if m_part[hh] is None else jnp.maximum(m_part[hh], pm)
            if j < n_b:
                for hh in heads:
                    p = jnp.exp2(s_b[hh, blk, :] - m_b[hh]).astype(BF16)
                    p_scr[hh, blk, :] = p
            if n_b and j == n_b - 1:
                kend = n_b * tq
                outs = []
                for hh in heads:
                    o_aug = _dot(vt_ref[0, hh, :, 0:kend], p_scr[hh, 0:kend, :])
                    outs.append(o_aug[0:V_HEAD] / o_aug[V_HEAD:V_HEAD + 1])
                o = jnp.concatenate(outs, axis=0).T
                o_ref[0, 0, slot] = (o * sz_ref[0, 0, slot].astype(F32)).astype(BF16)
        if n_a:
            for hh in heads:
                m_a[hh] = jnp.max(m_part[hh], axis=0, keepdims=True)

    def step(u):
        for slot in range(2):
            tile = lambda w: w if slot == 0 else nt - 1 - w
            v_a = tile(u) if u < n_pair else None
            v_b = tile(u - 1) if u > 0 else None
            streams(slot, v_a, v_b, s_bufs[u % 2][slot], s_bufs[(u - 1) % 2][slot])

    for u in range(n_pair + 1):
        pl.when(u_id == u)(functools.partial(step, u))


def _pair_index(s, nt):
    return jnp.minimum(s, nt - 1 - s), s // (nt // 2)


def _mla_attn_call(qt, k, vt, sz):
    bsz, heads, n_pair, _, _, ts = qt.shape
    seq = 2 * n_pair * ts
    wide = HEAD_GROUP * V_HEAD
    lo_rows, hi_rows = n_pair * ts, seq
    prev = lambda u: jnp.maximum(u - 1, 0)
    io_spec = pl.BlockSpec((1, 1, 2, ts, wide), lambda b, p, u: (b, prev(u), 0, 0, p))
    s_shapes = [pltpu.VMEM((HEAD_GROUP, rows, ts), F32) for _ in range(2) for rows in (lo_rows, hi_rows)]
    m_shapes = [pltpu.VMEM((HEAD_GROUP, 1, ts), F32) for _ in range(4)]
    p_shapes = [pltpu.VMEM((HEAD_GROUP, rows, ts), BF16) for rows in (lo_rows, hi_rows)]
    return pl.pallas_call(
        _mla_attn_kernel,
        grid=(bsz, heads // HEAD_GROUP, n_pair + 1),
        in_specs=[
            pl.BlockSpec((1, HEAD_GROUP, 1, 2, QK_PAD, ts),
                         lambda b, p, u: (b, p, jnp.minimum(u, n_pair - 1), 0, 0, 0)),
            pl.BlockSpec((1, HEAD_GROUP, seq, QK_PAD), lambda b, p, u: (b, p, 0, 0)),
            pl.BlockSpec((1, HEAD_GROUP, V_AUG, seq), lambda b, p, u: (b, p, 0, 0)),
            io_spec,
        ],
        out_specs=io_spec,
        out_shape=jax.ShapeDtypeStruct((bsz, n_pair, 2, ts, heads * V_HEAD), BF16),
        scratch_shapes=s_shapes + m_shapes + p_shapes,
        compiler_params=_params(3),
        name="mla_attn",
    )(qt, k, vt, sz)


def _mla_out_kernel(x_ref, o_ref_in, mod_ref, post_g_ref, w_out_ref, out_ref):
    y = _dot(o_ref_in[0, 0, 0], w_out_ref[...])
    out_ref[0] = _postnorm_residual(x_ref[0], y, mod_ref, post_g_ref)


def _mla_out_call(x, o, mod, post_g, w_out):
    bsz, seq, d = x.shape
    ts = SEQ_TILE
    nt = seq // ts
    tile = pl.BlockSpec((1, ts, d), lambda b, s: (b, s, 0))
    return pl.pallas_call(
        _mla_out_kernel,
        grid=(bsz, nt),
        in_specs=[
            tile,
            pl.BlockSpec((1, 1, 1, ts, d), lambda b, s: (b, *_pair_index(s, nt), 0, 0)),
            pl.BlockSpec((1, 3, d), lambda b, s: (b, 0, 0)),
            _const_spec((1, d)),
            _const_spec(w_out.shape),
        ],
        out_specs=tile,
        out_shape=jax.ShapeDtypeStruct(x.shape, F32),
        compiler_params=_params(2),
        name="mla_out",
    )(x, o, mod, post_g.reshape(1, d), w_out.astype(BF16))


def kernel(x, c, positions, ada_w, ada_b, pre_g, post_g, ev_w_in, ev_conv_w, ev_conv_b, ev_ln_g, ev_ln_b, ev_lru_conv_w, ev_lru_conv_b, ev_lru_wa, ev_lru_ba, ev_lru_wx, ev_lru_bx, ev_lru_lam, ev_w_out, od_w_in, od_q_norm, od_kv_norm, od_w_uq, od_w_ukv, od_w_out):
    depth = ada_w.shape[0]
    bsz, seq, d = x.shape
    assert d == D_MODEL and seq % SEQ_TILE == 0
    mod = _ada_call(c, ada_w, ada_b).reshape(depth, bsz, 3, d)
    cn, sn, ct, st = _rope_call(positions)
    for layer in range(depth):
        j = layer // 2
        if layer % 2 == 0:
            x = _even_call(x, mod[layer], pre_g[layer], post_g[layer], ev_w_in[j], ev_conv_w[j],
                           ev_conv_b[j], ev_ln_g[j], ev_ln_b[j], ev_lru_conv_w[j], ev_lru_conv_b[j],
                           ev_lru_wa[j], ev_lru_ba[j], ev_lru_wx[j], ev_lru_bx[j], ev_lru_lam[j],
                           ev_w_out[j])
        else:
            qt, k, vt, sz = _mla_proj_call(x, mod[layer], pre_g[layer], od_w_in[j], od_q_norm[j],
                                           od_kv_norm[j], od_w_uq[j], od_w_ukv[j], cn, sn, ct, st)
            o = _mla_attn_call(qt, k, vt, sz)
            x = _mla_out_call(x, o, mod[layer], post_g[layer], od_w_out[j])
    return x
```

```python
import functools

import jax
import jax.numpy as jnp
from jax import lax
from jax.experimental import pallas as pl
from jax.experimental.pallas import tpu as pltpu

F32 = jnp.float32
BF16 = jnp.bfloat16

D_MODEL = 1024
CONV_KERNEL = 31
LRU_CONV_KERNEL = 4
LRU_HEADS = 16
LRU_HEAD_DIM = D_MODEL // LRU_HEADS
LRU_C = 8.0
MLA_HEADS = 16
QK_NOPE = 64
QK_ROPE = 32
V_HEAD = 64
Q_LORA = 256
KV_LORA = 256
ROPE_THETA = 10000.0
EPS = 1e-6

LANE = 128
SUBLANE = 8
MXU_DIM = 256

SEQ_TILE = 256
EVEN_TILE = 512
N_SLAB = D_MODEL // LANE
CONV_HALO = 32
LRU_HALO = 8
CONV_TAPS_PAD = 32
GATE_GROUP = MXU_DIM
N_GATE_GROUP = D_MODEL // GATE_GROUP
CONV_ROWS = 64
QK_PAD = LANE
HEAD_GROUP = 4
MAX_PART = 32
V_AUG = V_HEAD + 16
LOG2_E = 1.4426950408889634
NEG_BIG = -1e30

VMEM_LIMIT_BYTES = 56 * 1024 * 1024


def _dot(a, b):
    return jnp.dot(a, b, preferred_element_type=F32)


def _dot_nt(a, b):
    return lax.dot_general(a, b, (((1,), (1,)), ((), ())), preferred_element_type=F32)


def _sigmoid(x):
    return 0.5 * jnp.tanh(0.5 * x) + 0.5


def _silu(x):
    h = 0.5 * x
    return h + h * jnp.tanh(h)


def _const_spec(shape):
    nd = len(shape)
    return pl.BlockSpec(shape, lambda *_: (0,) * nd, pipeline_mode=pl.Buffered(1))


def _params(n_grid):
    return pltpu.CompilerParams(
        dimension_semantics=("arbitrary",) * n_grid, vmem_limit_bytes=VMEM_LIMIT_BYTES)


def _ada_kernel(c_ref, w_ref, b_ref, o_ref):
    c = c_ref[...]
    o_ref[0] = jnp.dot(_silu(c), w_ref[0], preferred_element_type=F32,
                       precision=lax.Precision.HIGHEST) + b_ref[0]


def _ada_call(c, ada_w, ada_b):
    depth, d, d3 = ada_w.shape
    bsz = c.shape[0]
    n_col = d3 // d
    return pl.pallas_call(
        _ada_kernel,
        grid=(depth, n_col),
        in_specs=[
            pl.BlockSpec((bsz, d), lambda l, j: (0, 0)),
            pl.BlockSpec((1, d, d), lambda l, j: (l, 0, j)),
            pl.BlockSpec((1, 1, d), lambda l, j: (l, 0, j)),
        ],
        out_specs=pl.BlockSpec((1, bsz, d), lambda l, j: (l, 0, j)),
        out_shape=jax.ShapeDtypeStruct((depth, bsz, d3), F32),
        compiler_params=_params(2),
        name="ada_mod",
    )(c, ada_w, ada_b.reshape(depth, 1, d3))


def _rope_kernel(posrep_ref, invrep_ref, pos_ref, invcol_ref, cn_ref, sn_ref, ct_ref, st_ref):
    ang_n = posrep_ref[0].astype(F32) * invrep_ref[...]
    cn_ref[0] = jnp.cos(ang_n)
    sn_ref[0] = jnp.sin(ang_n)
    ang_t = invcol_ref[...] * pos_ref[0].astype(F32)
    ct_ref[0] = jnp.cos(ang_t)
    st_ref[0] = jnp.sin(ang_t)


def _rope_call(positions):
    bsz, seq = positions.shape
    half = QK_ROPE // 2
    rep = LANE // half
    inv = ROPE_THETA ** (-jnp.arange(0, QK_ROPE, 2, dtype=F32) / QK_ROPE)
    posrep = jnp.repeat(positions, half, axis=1).reshape(bsz, seq // rep, LANE)
    invrep = jnp.tile(inv, rep).reshape(1, LANE)
    n_spec = pl.BlockSpec((1, seq // rep, LANE), lambda b: (b, 0, 0))
    t_spec = pl.BlockSpec((1, half, seq), lambda b: (b, 0, 0))
    cn, sn, ct, st = pl.pallas_call(
        _rope_kernel,
        grid=(bsz,),
        in_specs=[
            n_spec,
            pl.BlockSpec((1, LANE), lambda b: (0, 0)),
            pl.BlockSpec((1, 1, seq), lambda b: (b, 0, 0)),
            pl.BlockSpec((half, 1), lambda b: (0, 0)),
        ],
        out_specs=[n_spec, n_spec, t_spec, t_spec],
        out_shape=[jax.ShapeDtypeStruct((bsz, seq // rep, LANE), F32)] * 2
        + [jax.ShapeDtypeStruct((bsz, half, seq), F32)] * 2,
        compiler_params=_params(1),
        name="rope_tables",
    )(posrep, invrep, positions.reshape(bsz, 1, seq), inv.reshape(half, 1))
    return cn.reshape(bsz, seq, half), sn.reshape(bsz, seq, half), ct, st


def _prenorm(x, mod_ref, pre_g_ref):
    ms = jnp.mean(x * x, axis=-1, keepdims=True)
    shift = mod_ref[0, 0:1, :]
    scale = mod_ref[0, 1:2, :]
    gmul = pre_g_ref[...] * (1.0 + scale)
    return x * lax.rsqrt(ms + EPS) * gmul + shift


def _postnorm_residual(x, y, mod_ref, post_g_ref):
    ms = jnp.mean(y * y, axis=-1, keepdims=True)
    gate = mod_ref[0, 2:3, :]
    return x + (gate * post_g_ref[...]) * (y * lax.rsqrt(ms + EPS))


def _scan_slab(a, b, carry):
    t = a.shape[0]
    groups = t // SUBLANE
    a = a.reshape(groups, SUBLANE, LANE)
    b = b.reshape(groups, SUBLANE, LANE)
    row = lax.broadcasted_iota(jnp.int32, a.shape, 1)
    d = 1
    while d < SUBLANE:
        ok = row >= d
        b = jnp.where(ok, a, 0.0) * pltpu.roll(b, d, axis=1) + b
        a = a * jnp.where(ok, pltpu.roll(a, d, axis=1), 1.0)
        d *= 2
    outs = []
    c = carry
    for v in range(groups):
        hv = b[v] + a[v] * c
        outs.append(hv)
        c = jnp.broadcast_to(hv[SUBLANE - 1:SUBLANE, :], (SUBLANE, LANE))
    return jnp.concatenate(outs, axis=0), c


def _even_kernel(x_ref, mod_ref, pre_g_ref, post_g_ref, w_vg_ref, w_rest_ref, cw_ref, cb_ref, lng_ref,
                 lnb_ref, lcw_ref, lcb_ref, wg_ref, ba_ref, bx_ref, lam_ref, w_out_ref, o_ref,
                 hbf, u0, u1, u2, ur, cbuf, cv, lbuf, xcf, xcb, a_s, b_s, hcar, act):
    ts = x_ref.shape[1]
    d = D_MODEL
    gw = GATE_GROUP
    spg = gw // LANE

    def rest(kind, c):
        lo = kind * gw + (c % spg) * LANE
        return ur[c // spg, :, lo:lo + LANE]

    @pl.when(pl.program_id(1) == 0)
    def _():
        cbuf[:, 0:CONV_HALO, :] = jnp.zeros((N_SLAB, CONV_HALO, LANE), F32)
        lbuf[:, 0:LRU_HALO, :] = jnp.zeros((N_SLAB, LRU_HALO, LANE), F32)
        hcar[...] = jnp.zeros(hcar.shape, F32)

    hbf[...] = _prenorm(x_ref[0], mod_ref, pre_g_ref).astype(BF16)

    u0[...] = _dot(hbf[...], w_vg_ref[:, 0:d])
    u1[...] = _dot(hbf[...], w_vg_ref[:, d:2 * d])
    u2[...] = _dot(hbf[...], w_vg_ref[:, 2 * d:3 * d])
    for g in range(N_GATE_GROUP):
        ur[g] = _dot(hbf[...], w_rest_ref[g])
    for c in range(N_SLAB):
        sl = slice(c * LANE, (c + 1) * LANE)
        cbuf[c, CONV_HALO:CONV_HALO + ts, :] = u0[:, sl] * _sigmoid(u1[:, sl])

    first = CONV_HALO - (CONV_KERNEL - 1)
    tap_groups = [[j for j in range(CONV_KERNEL) if (first + j) % SUBLANE == r] for r in range(SUBLANE)]

    def conv_slab(c, carry):
        w = cw_ref[c]
        for r0 in range(0, ts, CONV_ROWS):
            acc = jnp.broadcast_to(cb_ref[c], (CONV_ROWS, LANE))
            for taps in tap_groups:
                start = first + taps[0] + r0
                win = cbuf[c, start:start + CONV_ROWS + SUBLANE * (len(taps) - 1), :]
                for q, j in enumerate(taps):
                    acc = acc + win[SUBLANE * q:SUBLANE * q + CONV_ROWS] * w[j:j + 1, :]
            cv[c, r0:r0 + CONV_ROWS, :] = acc
        return carry

    lax.fori_loop(0, N_SLAB, conv_slab, 0)

    tot = cv[0]
    for c in range(1, N_SLAB):
        tot = tot + cv[c]
    mu = jnp.broadcast_to(jnp.sum(tot, axis=-1, keepdims=True) * (1.0 / d), (ts, LANE))
    sq = jnp.zeros((ts, LANE), F32)
    for c in range(N_SLAB):
        dc = cv[c] - mu
        sq = sq + dc * dc
    rstd = lax.rsqrt(jnp.sum(sq, axis=-1, keepdims=True) * (1.0 / d) + EPS)
    rstd = jnp.broadcast_to(rstd, (ts, LANE))
    for c in range(N_SLAB):
        sl = slice(c * LANE, (c + 1) * LANE)
        yn = (cv[c] - mu) * rstd * lng_ref[:, sl] + lnb_ref[:, sl]
        act[:, sl] = (_silu(yn) * _silu(rest(0, c))).astype(BF16)

    for c in range(N_SLAB):
        sl = slice(c * LANE, (c + 1) * LANE)
        lbuf[c, LRU_HALO:LRU_HALO + ts, :] = u2[:, sl]
        acc = jnp.broadcast_to(lcb_ref[:, sl], (ts, LANE))
        for j in range(LRU_CONV_KERNEL):
            start = LRU_HALO - (LRU_CONV_KERNEL - 1) + j
            acc = acc + lbuf[c, start:start + ts, :] * lcw_ref[j:j + 1, sl]
        xcf[:, sl] = acc
        xcb[:, sl] = acc.astype(BF16)

    nl = -lam_ref[...]
    softplus = jnp.maximum(nl, 0.0) + jnp.log1p(jnp.exp(-jnp.abs(nl)))
    for g in range(N_GATE_GROUP):
        sl = slice(g * gw, (g + 1) * gw)
        rg = _dot(xcb[:, sl], wg_ref[g])
        r = _sigmoid(rg[:, 0:gw] + ba_ref[:, sl])
        i = _sigmoid(rg[:, gw:2 * gw] + bx_ref[:, sl])
        a = jnp.exp((-LRU_C) * r * softplus[:, sl])
        a_s[:, sl] = a
        b_s[:, sl] = jnp.sqrt(1.0 - a * a) * i * xcf[:, sl]

    for c in range(N_SLAB):
        sl = slice(c * LANE, (c + 1) * LANE)
        h, c_new = _scan_slab(a_s[:, sl], b_s[:, sl], hcar[:, sl])
        hcar[:, sl] = c_new
        act[:, d + c * LANE:d + (c + 1) * LANE] = (h * _silu(rest(1, c))).astype(BF16)

    cbuf[:, 0:CONV_HALO, :] = cbuf[:, ts:ts + CONV_HALO, :]
    lbuf[:, 0:LRU_HALO, :] = lbuf[:, ts:ts + LRU_HALO, :]

    y = _dot(act[...], w_out_ref[...])
    o_ref[0] = _postnorm_residual(x_ref[0], y, mod_ref, post_g_ref)


def _even_call(x, mod, pre_g, post_g, w_in, conv_w, conv_b, ln_g, ln_b, lcw, lcb,
               wa, ba, wx, bx, lam, w_out):
    bsz, seq, d = x.shape
    ts = EVEN_TILE
    row = lambda v: v.reshape(1, d)
    cw = jnp.pad(conv_w, ((0, CONV_TAPS_PAD - CONV_KERNEL), (0, 0)))
    cw = cw.reshape(CONV_TAPS_PAD, N_SLAB, LANE).transpose(1, 0, 2)
    cb = conv_b.reshape(N_SLAB, 1, LANE)
    hpg = GATE_GROUP // LRU_HEAD_DIM
    eye = jnp.eye(hpg, dtype=F32)

    def blockdiag(w):
        wg = w.reshape(N_GATE_GROUP, hpg, LRU_HEAD_DIM, LRU_HEAD_DIM)
        return jnp.einsum('ghij,hk->ghikj', wg, eye).reshape(N_GATE_GROUP, GATE_GROUP, GATE_GROUP)

    wg = jnp.concatenate([blockdiag(wa), blockdiag(wx)], axis=-1).astype(BF16)
    ng, gw = N_GATE_GROUP, GATE_GROUP
    w_vg = jnp.concatenate([w_in[:, 0:2 * d], w_in[:, 3 * d:4 * d]], axis=1).astype(BF16)
    grouped = lambda lo: w_in[:, lo * d:(lo + 1) * d].reshape(d, ng, gw)
    w_rest = jnp.concatenate([grouped(2), grouped(4)], axis=2)
    w_rest = w_rest.transpose(1, 0, 2).astype(BF16)

    tile = pl.BlockSpec((1, ts, d), lambda b, s: (b, s, 0))
    scratch = [
        pltpu.VMEM((ts, d), BF16),
        pltpu.VMEM((ts, d), F32),
        pltpu.VMEM((ts, d), F32),
        pltpu.VMEM((ts, d), F32),
        pltpu.VMEM((ng, ts, 2 * gw), F32),
        pltpu.VMEM((N_SLAB, CONV_HALO + ts, LANE), F32),
        pltpu.VMEM((N_SLAB, ts, LANE), F32),
        pltpu.VMEM((N_SLAB, LRU_HALO + ts, LANE), F32),
        pltpu.VMEM((ts, d), F32),
        pltpu.VMEM((ts, d), BF16),
        pltpu.VMEM((ts, d), F32),
        pltpu.VMEM((ts, d), F32),
        pltpu.VMEM((SUBLANE, d), F32),
        pltpu.VMEM((ts, 2 * d), BF16),
    ]
    return pl.pallas_call(
        _even_kernel,
        grid=(bsz, seq // ts),
        in_specs=[
            tile,
            pl.BlockSpec((1, 3, d), lambda b, s: (b, 0, 0)),
            _const_spec((1, d)), _const_spec((1, d)),
            _const_spec(w_vg.shape), _const_spec(w_rest.shape),
            _const_spec(cw.shape), _const_spec(cb.shape),
            _const_spec((1, d)), _const_spec((1, d)),
            _const_spec(lcw.shape), _const_spec((1, d)),
            _const_spec(wg.shape),
            _const_spec((1, d)), _const_spec((1, d)), _const_spec((1, d)),
            _const_spec(w_out.shape),
        ],
        out_specs=tile,
        out_shape=jax.ShapeDtypeStruct(x.shape, F32),
        scratch_shapes=scratch,
        compiler_params=_params(2),
        name="even_layer",
    )(x, mod, row(pre_g), row(post_g), w_vg, w_rest, cw, cb, row(ln_g), row(ln_b),
      lcw, row(lcb), wg, row(ba), row(bx), row(lam), w_out.astype(BF16))


def _mla_proj_kernel(x_ref, mod_ref, pre_g_ref, w_c_ref, w_z_ref, qn_ref, kvn_ref,
                     wuq_t_ref, wk_ref, wv_t_ref, cn_ref, sn_ref, ct_ref, st_ref,
                     qt_ref, k_ref, vt_ref, sz_ref, hbf):
    ts = x_ref.shape[1]
    half = QK_ROPE // 2
    hbf[...] = _prenorm(x_ref[0], mod_ref, pre_g_ref).astype(BF16)

    uc = _dot(hbf[...], w_c_ref[...])
    sz_ref[0, 0, 0] = _silu(_dot(hbf[...], w_z_ref[...])).astype(BF16)

    def rms(v, g_ref):
        ms = jnp.mean(v * v, axis=-1, keepdims=True)
        return (v * lax.rsqrt(ms + EPS) * g_ref[...]).astype(BF16)

    cqn = rms(uc[:, 0:Q_LORA], qn_ref)
    ckvn = rms(uc[:, Q_LORA:Q_LORA + KV_LORA], kvn_ref)
    kr = uc[:, Q_LORA + KV_LORA:Q_LORA + KV_LORA + QK_ROPE]

    q_t = _dot_nt(wuq_t_ref[...], cqn)
    c_t = ct_ref[0]
    s_t = st_ref[0]
    scale = (QK_NOPE + QK_ROPE) ** -0.5 * LOG2_E
    for hd in range(MLA_HEADS):
        base = hd * QK_PAD
        x1 = q_t[base + QK_NOPE:base + QK_NOPE + half]
        x2 = q_t[base + QK_NOPE + half:base + QK_NOPE + QK_ROPE]
        blk = jnp.concatenate([
            q_t[base:base + QK_NOPE],
            x1 * c_t - x2 * s_t,
            x1 * s_t + x2 * c_t,
            q_t[base + QK_NOPE + QK_ROPE:base + QK_PAD],
        ], axis=0)
        qt_ref[0, hd, 0, 0] = (blk * scale).astype(BF16)

    kv_k = _dot(ckvn, wk_ref[...])
    c_n = cn_ref[0]
    s_n = sn_ref[0]
    k1 = kr[:, 0:half]
    k2 = kr[:, half:QK_ROPE]
    kr128 = jnp.concatenate([
        jnp.zeros((ts, QK_NOPE), F32),
        k1 * c_n - k2 * s_n,
        k1 * s_n + k2 * c_n,
        jnp.zeros((ts, QK_PAD - QK_NOPE - QK_ROPE), F32),
    ], axis=-1)
    for hd in range(MLA_HEADS):
        k_ref[0, hd] = (kv_k[:, hd * QK_PAD:(hd + 1) * QK_PAD] + kr128).astype(BF16)

    v_t = _dot_nt(wv_t_ref[...], ckvn)
    for hd in range(MLA_HEADS):
        vt_ref[0, hd, 0:V_HEAD, :] = v_t[hd * V_HEAD:(hd + 1) * V_HEAD].astype(BF16)
        row = lax.broadcasted_iota(jnp.int32, (V_AUG - V_HEAD, ts), 0)
        vt_ref[0, hd, V_HEAD:V_AUG, :] = jnp.where(row == 0, 1.0, 0.0).astype(BF16)


def _mla_proj_call(x, mod, pre_g, w_in, q_norm, kv_norm, w_uq, w_ukv, cn, sn, ct, st):
    bsz, seq, d = x.shape
    ts = SEQ_TILE
    nt = seq // ts
    half = QK_ROPE // 2
    n_c = Q_LORA + KV_LORA + QK_ROPE
    w_c = jnp.pad(w_in[:, :n_c], ((0, 0), (0, (-n_c) % LANE))).astype(BF16)
    w_z = w_in[:, n_c:].astype(BF16)
    wq = w_uq.reshape(Q_LORA, MLA_HEADS, QK_NOPE + QK_ROPE)
    wq = jnp.pad(wq, ((0, 0), (0, 0), (0, QK_PAD - QK_NOPE - QK_ROPE)))
    wuq_t = wq.reshape(Q_LORA, MLA_HEADS * QK_PAD).T.astype(BF16)
    wkv = w_ukv.reshape(KV_LORA, MLA_HEADS, QK_NOPE + V_HEAD)
    wk = jnp.pad(wkv[:, :, :QK_NOPE], ((0, 0), (0, 0), (0, QK_PAD - QK_NOPE)))
    wk = wk.reshape(KV_LORA, MLA_HEADS * QK_PAD).astype(BF16)
    wv_t = wkv[:, :, QK_NOPE:].reshape(KV_LORA, MLA_HEADS * V_HEAD).T.astype(BF16)

    tile = pl.BlockSpec((1, ts, d), lambda b, s: (b, s, 0))
    return pl.pallas_call(
        _mla_proj_kernel,
        grid=(bsz, nt),
        in_specs=[
            tile,
            pl.BlockSpec((1, 3, d), lambda b, s: (b, 0, 0)),
            _const_spec((1, d)),
            _const_spec(w_c.shape), _const_spec(w_z.shape),
            _const_spec((1, Q_LORA)), _const_spec((1, KV_LORA)),
            _const_spec(wuq_t.shape), _const_spec(wk.shape), _const_spec(wv_t.shape),
            pl.BlockSpec((1, ts, half), lambda b, s: (b, s, 0)),
            pl.BlockSpec((1, ts, half), lambda b, s: (b, s, 0)),
            pl.BlockSpec((1, half, ts), lambda b, s: (b, 0, s)),
            pl.BlockSpec((1, half, ts), lambda b, s: (b, 0, s)),
        ],
        out_specs=[
            pl.BlockSpec((1, MLA_HEADS, 1, 1, QK_PAD, ts),
                         lambda b, s: (b, 0, *_pair_index(s, nt), 0, 0)),
            pl.BlockSpec((1, MLA_HEADS, ts, QK_PAD), lambda b, s: (b, 0, s, 0)),
            pl.BlockSpec((1, MLA_HEADS, V_AUG, ts), lambda b, s: (b, 0, 0, s)),
            pl.BlockSpec((1, 1, 1, ts, d), lambda b, s: (b, *_pair_index(s, nt), 0, 0)),
        ],
        out_shape=[
            jax.ShapeDtypeStruct((bsz, MLA_HEADS, nt // 2, 2, QK_PAD, ts), BF16),
            jax.ShapeDtypeStruct((bsz, MLA_HEADS, seq, QK_PAD), BF16),
            jax.ShapeDtypeStruct((bsz, MLA_HEADS, V_AUG, seq), BF16),
            jax.ShapeDtypeStruct((bsz, nt // 2, 2, ts, d), BF16),
        ],
        scratch_shapes=[pltpu.VMEM((ts, d), BF16)],
        compiler_params=_params(2),
        name="mla_proj",
    )(x, mod, pre_g.reshape(1, d), w_c, w_z, q_norm.reshape(1, Q_LORA), kv_norm.reshape(1, KV_LORA),
      wuq_t, wk, wv_t, cn, sn, ct, st)


def _mla_attn_kernel(qt_ref, k_ref, vt_ref, sz_ref, o_ref,
                     s_lo0, s_hi0, s_lo1, s_hi1, m_lo0, m_hi0, m_lo1, m_hi1, p_lo, p_hi):
    tq = qt_ref.shape[-1]
    nt = k_ref.shape[2] // tq
    n_pair = nt // 2
    u_id = pl.program_id(2)
    heads = range(HEAD_GROUP)
    s_bufs = (((s_lo0, m_lo0), (s_hi0, m_hi0)), ((s_lo1, m_lo1), (s_hi1, m_hi1)))
    p_bufs = (p_lo, p_hi)

    def streams(slot, v_a, v_b, bufs_a, bufs_b):
        n_a = v_a + 1 if v_a is not None else 0
        n_b = v_b + 1 if v_b is not None else 0
        s_a, m_a = bufs_a
        s_b, m_bref = bufs_b
        p_scr = p_bufs[slot]
        key = lax.broadcasted_iota(jnp.int32, (tq, tq), 0)
        qry = lax.broadcasted_iota(jnp.int32, (tq, tq), 1)
        q_t = [qt_ref[0, hh, 0, slot] for hh in heads] if n_a else None
        m_b = [m_bref[hh] for hh in heads] if n_b else None
        m_part = [None] * HEAD_GROUP
        for j in range(max(n_a, n_b)):
            blk = slice(j * tq, (j + 1) * tq)
            if j < n_a:
                for hh in heads:
                    s = _dot(k_ref[0, hh, blk, :], q_t[hh])
                    if j == v_a:
                        s = jnp.where(key <= qry, s, NEG_BIG)
                    s_a[hh, blk, :] = s
                    pm = jnp.max(s.reshape(tq // MAX_PART, MAX_PART, tq), axis=0)
                    m_part[hh] = pm if m_part[hh] is None else jnp.maximum(m_part[hh], pm)
            if j < n_b:
                for hh in heads:
                    p = jnp.exp2(s_b[hh, blk, :] - m_b[hh]).astype(BF16)
                    p_scr[hh, blk, :] = p
            if n_b and j == n_b - 1:
                kend = n_b * tq
                outs = []
                for hh in heads:
                    o_aug = _dot(vt_ref[0, hh, :, 0:kend], p_scr[hh, 0:kend, :])
                    outs.append(o_aug[0:V_HEAD] / o_aug[V_HEAD:V_HEAD + 1])
                o = jnp.concatenate(outs, axis=0).T
                o_ref[0, 0, slot] = (o * sz_ref[0, 0, slot].astype(F32)).astype(BF16)
        if n_a:
            for hh in heads:
                m_a[hh] = jnp.max(m_part[hh], axis=0, keepdims=True)

    def step(u):
        for slot in range(2):
            tile = lambda w: w if slot == 0 else nt - 1 - w
            v_a = tile(u) if u < n_pair else None
            v_b = tile(u - 1) if u > 0 else None
            streams(slot, v_a, v_b, s_bufs[u % 2][slot], s_bufs[(u - 1) % 2][slot])

    for u in range(n_pair + 1):
        pl.when(u_id == u)(functools.partial(step, u))


def _pair_index(s, nt):
    return jnp.minimum(s, nt - 1 - s), s // (nt // 2)


def _mla_attn_call(qt, k, vt, sz):
    bsz, heads, n_pair, _, _, ts = qt.shape
    seq = 2 * n_pair * ts
    wide = HEAD_GROUP * V_HEAD
    lo_rows, hi_rows = n_pair * ts, seq
    prev = lambda u: jnp.maximum(u - 1, 0)
    io_spec = pl.BlockSpec((1, 1, 2, ts, wide), lambda b, p, u: (b, prev(u), 0, 0, p))
    s_shapes = [pltpu.VMEM((HEAD_GROUP, rows, ts), F32) for _ in range(2) for rows in (lo_rows, hi_rows)]
    m_shapes = [pltpu.VMEM((HEAD_GROUP, 1, ts), F32) for _ in range(4)]
    p_shapes = [pltpu.VMEM((HEAD_GROUP, rows, ts), BF16) for rows in (lo_rows, hi_rows)]
    return pl.pallas_call(
        _mla_attn_kernel,
        grid=(bsz, heads // HEAD_GROUP, n_pair + 1),
        in_specs=[
            pl.BlockSpec((1, HEAD_GROUP, 1, 2, QK_PAD, ts),
                         lambda b, p, u: (b, p, jnp.minimum(u, n_pair - 1), 0, 0, 0)),
            pl.BlockSpec((1, HEAD_GROUP, seq, QK_PAD), lambda b, p, u: (b, p, 0, 0)),
            pl.BlockSpec((1, HEAD_GROUP, V_AUG, seq), lambda b, p, u: (b, p, 0, 0)),
            io_spec,
        ],
        out_specs=io_spec,
        out_shape=jax.ShapeDtypeStruct((bsz, n_pair, 2, ts, heads * V_HEAD), BF16),
        scratch_shapes=s_shapes + m_shapes + p_shapes,
        compiler_params=_params(3),
        name="mla_attn",
    )(qt, k, vt, sz)


def _mla_out_kernel(x_ref, o_ref_in, mod_ref, post_g_ref, w_out_ref, out_ref):
    y = _dot(o_ref_in[0, 0, 0], w_out_ref[...])
    out_ref[0] = _postnorm_residual(x_ref[0], y, mod_ref, post_g_ref)


def _mla_out_call(x, o, mod, post_g, w_out):
    bsz, seq, d = x.shape
    ts = SEQ_TILE
    nt = seq // ts
    tile = pl.BlockSpec((1, ts, d), lambda b, s: (b, s, 0))
    return pl.pallas_call(
        _mla_out_kernel,
        grid=(bsz, nt),
        in_specs=[
            tile,
            pl.BlockSpec((1, 1, 1, ts, d), lambda b, s: (b, *_pair_index(s, nt), 0, 0)),
            pl.BlockSpec((1, 3, d), lambda b, s: (b, 0, 0)),
            _const_spec((1, d)),
            _const_spec(w_out.shape),
        ],
        out_specs=tile,
        out_shape=jax.ShapeDtypeStruct(x.shape, F32),
        compiler_params=_params(2),
        name="mla_out",
    )(x, o, mod, post_g.reshape(1, d), w_out.astype(BF16))


def kernel(x, c, positions, ada_w, ada_b, pre_g, post_g, ev_w_in, ev_conv_w, ev_conv_b, ev_ln_g, ev_ln_b, ev_lru_conv_w, ev_lru_conv_b, ev_lru_wa, ev_lru_ba, ev_lru_wx, ev_lru_bx, ev_lru_lam, ev_w_out, od_w_in, od_q_norm, od_kv_norm, od_w_uq, od_w_ukv, od_w_out):
    depth = ada_w.shape[0]
    bsz, seq, d = x.shape
    assert d == D_MODEL and seq % (2 * SEQ_TILE) == 0 and seq % EVEN_TILE == 0
    mod = _ada_call(c, ada_w, ada_b).reshape(depth, bsz, 3, d)
    cn, sn, ct, st = _rope_call(positions)
    for layer in range(depth):
        j = layer // 2
        if layer % 2 == 0:
            x = _even_call(x, mod[layer], pre_g[layer], post_g[layer], ev_w_in[j], ev_conv_w[j],
                           ev_conv_b[j], ev_ln_g[j], ev_ln_b[j], ev_lru_conv_w[j], ev_lru_conv_b[j],
                           ev_lru_wa[j], ev_lru_ba[j], ev_lru_wx[j], ev_lru_bx[j], ev_lru_lam[j],
                           ev_w_out[j])
        else:
            qt, k, vt, sz = _mla_proj_call(x, mod[layer], pre_g[layer], od_w_in[j], od_q_norm[j],
                                           od_kv_norm[j], od_w_uq[j], od_w_ukv[j], cn, sn, ct, st)
            o = _mla_attn_call(qt, k, vt, sz)
            x = _mla_out_call(x, o, mod[layer], post_g[layer], od_w_out[j])
    return x
```

```python
import functools

import jax
import jax.numpy as jnp
from jax import lax
from jax.experimental import pallas as pl
from jax.experimental.pallas import tpu as pltpu

F32 = jnp.float32
BF16 = jnp.bfloat16

D_MODEL = 1024
CONV_KERNEL = 31
LRU_CONV_KERNEL = 4
LRU_HEADS = 16
LRU_HEAD_DIM = D_MODEL // LRU_HEADS
LRU_C = 8.0
MLA_HEADS = 16
QK_NOPE = 64
QK_ROPE = 32
V_HEAD = 64
Q_LORA = 256
KV_LORA = 256
ROPE_THETA = 10000.0
EPS = 1e-6

LANE = 128
SUBLANE = 8
MXU_DIM = 256

SEQ_TILE = 256
EVEN_TILE = 512
N_SLAB = D_MODEL // LANE
CONV_HALO = 32
LRU_HALO = 8
CONV_TAPS_PAD = 32
GATE_GROUP = MXU_DIM
N_GATE_GROUP = D_MODEL // GATE_GROUP
CONV_ROWS = 64
QK_PAD = LANE
HEAD_GROUP = 4
MAX_PART = 32
V_AUG = V_HEAD + 16
LOG2_E = 1.4426950408889634
NEG_BIG = -1e30

VMEM_LIMIT_BYTES = 56 * 1024 * 1024


def _dot(a, b):
    return jnp.dot(a, b, preferred_element_type=F32)


def _dot_nt(a, b):
    return lax.dot_general(a, b, (((1,), (1,)), ((), ())), preferred_element_type=F32)


def _sigmoid(x):
    return 0.5 * jnp.tanh(0.5 * x) + 0.5


def _silu(x):
    h = 0.5 * x
    return h + h * jnp.tanh(h)


def _layer_spec(stacked, j):
    rest = stacked.shape[1:]
    return pl.BlockSpec((None,) + rest, lambda *_: (j,) + (0,) * len(rest),
                        pipeline_mode=pl.Buffered(1))


def _mod_spec(mod, layer):
    return pl.BlockSpec((None, 1) + mod.shape[2:], lambda b, *_: (layer, b, 0, 0))


def _params(n_grid):
    return pltpu.CompilerParams(
        dimension_semantics=("arbitrary",) * n_grid, vmem_limit_bytes=VMEM_LIMIT_BYTES)


def _ada_kernel(c_ref, w_ref, b_ref, o_ref):
    c = c_ref[...]
    o_ref[0] = jnp.dot(_silu(c), w_ref[0], preferred_element_type=F32,
                       precision=lax.Precision.HIGHEST) + b_ref[0]


def _ada_call(c, ada_w, ada_b):
    depth, d, d3 = ada_w.shape
    bsz = c.shape[0]
    n_col = d3 // d
    return pl.pallas_call(
        _ada_kernel,
        grid=(depth, n_col),
        in_specs=[
            pl.BlockSpec((bsz, d), lambda l, j: (0, 0)),
            pl.BlockSpec((1, d, d), lambda l, j: (l, 0, j)),
            pl.BlockSpec((1, 1, d), lambda l, j: (l, 0, j)),
        ],
        out_specs=pl.BlockSpec((1, bsz, d), lambda l, j: (l, 0, j)),
        out_shape=jax.ShapeDtypeStruct((depth, bsz, d3), F32),
        compiler_params=_params(2),
        name="ada_mod",
    )(c, ada_w, ada_b.reshape(depth, 1, d3))


def _rope_kernel(posrep_ref, invrep_ref, pos_ref, invcol_ref, cn_ref, sn_ref, ct_ref, st_ref):
    ang_n = posrep_ref[0].astype(F32) * invrep_ref[...]
    cn_ref[0] = jnp.cos(ang_n)
    sn_ref[0] = jnp.sin(ang_n)
    ang_t = invcol_ref[...] * pos_ref[0].astype(F32)
    ct_ref[0] = jnp.cos(ang_t)
    st_ref[0] = jnp.sin(ang_t)


def _rope_call(positions):
    bsz, seq = positions.shape
    half = QK_ROPE // 2
    rep = LANE // half
    inv = ROPE_THETA ** (-jnp.arange(0, QK_ROPE, 2, dtype=F32) / QK_ROPE)
    posrep = jnp.repeat(positions, half, axis=1).reshape(bsz, seq // rep, LANE)
    invrep = jnp.tile(inv, rep).reshape(1, LANE)
    n_spec = pl.BlockSpec((1, seq // rep, LANE), lambda b: (b, 0, 0))
    t_spec = pl.BlockSpec((1, half, seq), lambda b: (b, 0, 0))
    cn, sn, ct, st = pl.pallas_call(
        _rope_kernel,
        grid=(bsz,),
        in_specs=[
            n_spec,
            pl.BlockSpec((1, LANE), lambda b: (0, 0)),
            pl.BlockSpec((1, 1, seq), lambda b: (b, 0, 0)),
            pl.BlockSpec((half, 1), lambda b: (0, 0)),
        ],
        out_specs=[n_spec, n_spec, t_spec, t_spec],
        out_shape=[jax.ShapeDtypeStruct((bsz, seq // rep, LANE), F32)] * 2
        + [jax.ShapeDtypeStruct((bsz, half, seq), F32)] * 2,
        compiler_params=_params(1),
        name="rope_tables",
    )(posrep, invrep, positions.reshape(bsz, 1, seq), inv.reshape(half, 1))
    return cn.reshape(bsz, seq, half), sn.reshape(bsz, seq, half), ct, st


def _prenorm(x, mod_ref, pre_g_ref):
    ms = jnp.mean(x * x, axis=-1, keepdims=True)
    shift = mod_ref[0, 0:1, :]
    scale = mod_ref[0, 1:2, :]
    gmul = pre_g_ref[...] * (1.0 + scale)
    return x * lax.rsqrt(ms + EPS) * gmul + shift


def _postnorm_residual(x, y, mod_ref, post_g_ref):
    ms = jnp.mean(y * y, axis=-1, keepdims=True)
    gate = mod_ref[0, 2:3, :]
    return x + (gate * post_g_ref[...]) * (y * lax.rsqrt(ms + EPS))


def _scan_slab(a, b, carry):
    t = a.shape[0]
    groups = t // SUBLANE
    a = a.reshape(groups, SUBLANE, LANE)
    b = b.reshape(groups, SUBLANE, LANE)
    row = lax.broadcasted_iota(jnp.int32, a.shape, 1)
    d = 1
    while d < SUBLANE:
        ok = row >= d
        b = jnp.where(ok, a, 0.0) * pltpu.roll(b, d, axis=1) + b
        a = a * jnp.where(ok, pltpu.roll(a, d, axis=1), 1.0)
        d *= 2
    outs = []
    c = carry
    for v in range(groups):
        hv = b[v] + a[v] * c
        outs.append(hv)
        c = jnp.broadcast_to(hv[SUBLANE - 1:SUBLANE, :], (SUBLANE, LANE))
    return jnp.concatenate(outs, axis=0), c


def _even_kernel(x_ref, mod_ref, pre_g_ref, post_g_ref, w_vg_ref, w_rest_ref, cw_ref, cb_ref, lng_ref,
                 lnb_ref, lcw_ref, lcb_ref, wg_ref, ba_ref, bx_ref, lam_ref, w_out_ref, o_ref,
                 hbf, u0, u1, u2, ur, cbuf, cv, lbuf, xcf, xcb, a_s, b_s, hcar, act):
    ts = x_ref.shape[1]
    d = D_MODEL
    gw = GATE_GROUP
    spg = gw // LANE

    def rest(kind, c):
        lo = kind * gw + (c % spg) * LANE
        return ur[c // spg, :, lo:lo + LANE]

    @pl.when(pl.program_id(1) == 0)
    def _():
        cbuf[:, 0:CONV_HALO, :] = jnp.zeros((N_SLAB, CONV_HALO, LANE), F32)
        lbuf[:, 0:LRU_HALO, :] = jnp.zeros((N_SLAB, LRU_HALO, LANE), F32)
        hcar[...] = jnp.zeros(hcar.shape, F32)

    hbf[...] = _prenorm(x_ref[0], mod_ref, pre_g_ref).astype(BF16)

    u0[...] = _dot(hbf[...], w_vg_ref[:, 0:d])
    u1[...] = _dot(hbf[...], w_vg_ref[:, d:2 * d])
    u2[...] = _dot(hbf[...], w_vg_ref[:, 2 * d:3 * d])
    for g in range(N_GATE_GROUP):
        ur[g] = _dot(hbf[...], w_rest_ref[g])
    for c in range(N_SLAB):
        sl = slice(c * LANE, (c + 1) * LANE)
        cbuf[c, CONV_HALO:CONV_HALO + ts, :] = u0[:, sl] * _sigmoid(u1[:, sl])

    first = CONV_HALO - (CONV_KERNEL - 1)
    tap_groups = [[j for j in range(CONV_KERNEL) if (first + j) % SUBLANE == r] for r in range(SUBLANE)]

    def conv_slab(c, carry):
        w = cw_ref[c]
        for r0 in range(0, ts, CONV_ROWS):
            acc = jnp.broadcast_to(cb_ref[c], (CONV_ROWS, LANE))
            for taps in tap_groups:
                start = first + taps[0] + r0
                win = cbuf[c, start:start + CONV_ROWS + SUBLANE * (len(taps) - 1), :]
                for q, j in enumerate(taps):
                    acc = acc + win[SUBLANE * q:SUBLANE * q + CONV_ROWS] * w[j:j + 1, :]
            cv[c, r0:r0 + CONV_ROWS, :] = acc
        return carry

    lax.fori_loop(0, N_SLAB, conv_slab, 0)

    tot = cv[0]
    for c in range(1, N_SLAB):
        tot = tot + cv[c]
    mu = jnp.broadcast_to(jnp.sum(tot, axis=-1, keepdims=True) * (1.0 / d), (ts, LANE))
    sq = jnp.zeros((ts, LANE), F32)
    for c in range(N_SLAB):
        dc = cv[c] - mu
        sq = sq + dc * dc
    rstd = lax.rsqrt(jnp.sum(sq, axis=-1, keepdims=True) * (1.0 / d) + EPS)
    rstd = jnp.broadcast_to(rstd, (ts, LANE))
    for c in range(N_SLAB):
        sl = slice(c * LANE, (c + 1) * LANE)
        yn = (cv[c] - mu) * rstd * lng_ref[:, sl] + lnb_ref[:, sl]
        act[:, sl] = (_silu(yn) * _silu(rest(0, c))).astype(BF16)

    for c in range(N_SLAB):
        sl = slice(c * LANE, (c + 1) * LANE)
        lbuf[c, LRU_HALO:LRU_HALO + ts, :] = u2[:, sl]
        acc = jnp.broadcast_to(lcb_ref[:, sl], (ts, LANE))
        for j in range(LRU_CONV_KERNEL):
            start = LRU_HALO - (LRU_CONV_KERNEL - 1) + j
            acc = acc + lbuf[c, start:start + ts, :] * lcw_ref[j:j + 1, sl]
        xcf[:, sl] = acc
        xcb[:, sl] = acc.astype(BF16)

    nl = -lam_ref[...]
    softplus = jnp.maximum(nl, 0.0) + jnp.log1p(jnp.exp(-jnp.abs(nl)))
    for g in range(N_GATE_GROUP):
        sl = slice(g * gw, (g + 1) * gw)
        rg = _dot(xcb[:, sl], wg_ref[g])
        r = _sigmoid(rg[:, 0:gw] + ba_ref[:, sl])
        i = _sigmoid(rg[:, gw:2 * gw] + bx_ref[:, sl])
        a = jnp.exp((-LRU_C) * r * softplus[:, sl])
        a_s[:, sl] = a
        b_s[:, sl] = jnp.sqrt(1.0 - a * a) * i * xcf[:, sl]

    for c in range(N_SLAB):
        sl = slice(c * LANE, (c + 1) * LANE)
        h, c_new = _scan_slab(a_s[:, sl], b_s[:, sl], hcar[:, sl])
        hcar[:, sl] = c_new
        act[:, d + c * LANE:d + (c + 1) * LANE] = (h * _silu(rest(1, c))).astype(BF16)

    cbuf[:, 0:CONV_HALO, :] = cbuf[:, ts:ts + CONV_HALO, :]
    lbuf[:, 0:LRU_HALO, :] = lbuf[:, ts:ts + LRU_HALO, :]

    y = _dot(act[...], w_out_ref[...])
    o_ref[0] = _postnorm_residual(x_ref[0], y, mod_ref, post_g_ref)


def _even_prep(w_in, conv_w, conv_b, ln_g, ln_b, lcw, lcb, wa, ba, wx, bx, lam, w_out):
    n, d = w_in.shape[0], D_MODEL
    ng, gw = N_GATE_GROUP, GATE_GROUP
    row = lambda v: v.reshape(n, 1, d)
    cw = jnp.pad(conv_w, ((0, 0), (0, CONV_TAPS_PAD - CONV_KERNEL), (0, 0)))
    cw = cw.reshape(n, CONV_TAPS_PAD, N_SLAB, LANE).transpose(0, 2, 1, 3)
    cb = conv_b.reshape(n, N_SLAB, 1, LANE)
    hpg = gw // LRU_HEAD_DIM
    eye = jnp.eye(hpg, dtype=F32)

    def blockdiag(w):
        wg = w.reshape(n, ng, hpg, LRU_HEAD_DIM, LRU_HEAD_DIM)
        return jnp.einsum('lghij,hk->lghikj', wg, eye).reshape(n, ng, gw, gw)

    wg = jnp.concatenate([blockdiag(wa), blockdiag(wx)], axis=-1).astype(BF16)
    w_vg = jnp.concatenate([w_in[:, :, 0:2 * d], w_in[:, :, 3 * d:4 * d]], axis=2).astype(BF16)
    grouped = lambda lo: w_in[:, :, lo * d:(lo + 1) * d].reshape(n, d, ng, gw)
    w_rest = jnp.concatenate([grouped(2), grouped(4)], axis=3)
    w_rest = w_rest.transpose(0, 2, 1, 3).astype(BF16)
    return (w_vg, w_rest, cw, cb, row(ln_g), row(ln_b), lcw, row(lcb), wg,
            row(ba), row(bx), row(lam), w_out.astype(BF16))


def _even_call(x, mod, pre_g, post_g, layer, params):
    bsz, seq, d = x.shape
    ts = EVEN_TILE
    ng, gw = N_GATE_GROUP, GATE_GROUP
    j = layer // 2
    tile = pl.BlockSpec((1, ts, d), lambda b, s: (b, s, 0))
    scratch = [
        pltpu.VMEM((ts, d), BF16),
        pltpu.VMEM((ts, d), F32),
        pltpu.VMEM((ts, d), F32),
        pltpu.VMEM((ts, d), F32),
        pltpu.VMEM((ng, ts, 2 * gw), F32),
        pltpu.VMEM((N_SLAB, CONV_HALO + ts, LANE), F32),
        pltpu.VMEM((N_SLAB, ts, LANE), F32),
        pltpu.VMEM((N_SLAB, LRU_HALO + ts, LANE), F32),
        pltpu.VMEM((ts, d), F32),
        pltpu.VMEM((ts, d), BF16),
        pltpu.VMEM((ts, d), F32),
        pltpu.VMEM((ts, d), F32),
        pltpu.VMEM((SUBLANE, d), F32),
        pltpu.VMEM((ts, 2 * d), BF16),
    ]
    return pl.pallas_call(
        _even_kernel,
        grid=(bsz, seq // ts),
        in_specs=[tile, _mod_spec(mod, layer), _layer_spec(pre_g, layer), _layer_spec(post_g, layer)]
        + [_layer_spec(p, j) for p in params],
        out_specs=tile,
        out_shape=jax.ShapeDtypeStruct(x.shape, F32),
        scratch_shapes=scratch,
        compiler_params=_params(2),
        name="even_layer",
    )(x, mod, pre_g, post_g, *params)


def _mla_proj_kernel(x_ref, mod_ref, pre_g_ref, w_c_ref, w_z_ref, qn_ref, kvn_ref,
                     wuq_t_ref, wk_ref, wv_t_ref, cn_ref, sn_ref, ct_ref, st_ref,
                     qt_ref, k_ref, vt_ref, sz_ref, hbf):
    ts = x_ref.shape[1]
    half = QK_ROPE // 2
    hbf[...] = _prenorm(x_ref[0], mod_ref, pre_g_ref).astype(BF16)

    uc = _dot(hbf[...], w_c_ref[...])
    sz_ref[0, 0, 0] = _silu(_dot(hbf[...], w_z_ref[...])).astype(BF16)

    def rms(v, g_ref):
        ms = jnp.mean(v * v, axis=-1, keepdims=True)
        return (v * lax.rsqrt(ms + EPS) * g_ref[...]).astype(BF16)

    cqn = rms(uc[:, 0:Q_LORA], qn_ref)
    ckvn = rms(uc[:, Q_LORA:Q_LORA + KV_LORA], kvn_ref)
    kr = uc[:, Q_LORA + KV_LORA:Q_LORA + KV_LORA + QK_ROPE]

    q_t = _dot_nt(wuq_t_ref[...], cqn)
    c_t = ct_ref[0]
    s_t = st_ref[0]
    scale = (QK_NOPE + QK_ROPE) ** -0.5 * LOG2_E
    for hd in range(MLA_HEADS):
        base = hd * QK_PAD
        x1 = q_t[base + QK_NOPE:base + QK_NOPE + half]
        x2 = q_t[base + QK_NOPE + half:base + QK_NOPE + QK_ROPE]
        blk = jnp.concatenate([
            q_t[base:base + QK_NOPE],
            x1 * c_t - x2 * s_t,
            x1 * s_t + x2 * c_t,
            q_t[base + QK_NOPE + QK_ROPE:base + QK_PAD],
        ], axis=0)
        qt_ref[0, hd, 0, 0] = (blk * scale).astype(BF16)

    kv_k = _dot(ckvn, wk_ref[...])
    c_n = cn_ref[0]
    s_n = sn_ref[0]
    k1 = kr[:, 0:half]
    k2 = kr[:, half:QK_ROPE]
    kr128 = jnp.concatenate([
        jnp.zeros((ts, QK_NOPE), F32),
        k1 * c_n - k2 * s_n,
        k1 * s_n + k2 * c_n,
        jnp.zeros((ts, QK_PAD - QK_NOPE - QK_ROPE), F32),
    ], axis=-1)
    for hd in range(MLA_HEADS):
        k_ref[0, hd] = (kv_k[:, hd * QK_PAD:(hd + 1) * QK_PAD] + kr128).astype(BF16)

    v_t = _dot_nt(wv_t_ref[...], ckvn)
    for hd in range(MLA_HEADS):
        vt_ref[0, hd, 0:V_HEAD, :] = v_t[hd * V_HEAD:(hd + 1) * V_HEAD].astype(BF16)
        row = lax.broadcasted_iota(jnp.int32, (V_AUG - V_HEAD, ts), 0)
        vt_ref[0, hd, V_HEAD:V_AUG, :] = jnp.where(row == 0, 1.0, 0.0).astype(BF16)


def _mla_prep(w_in, q_norm, kv_norm, w_uq, w_ukv, w_out):
    n = w_in.shape[0]
    n_c = Q_LORA + KV_LORA + QK_ROPE
    w_c = jnp.pad(w_in[:, :, :n_c], ((0, 0), (0, 0), (0, (-n_c) % LANE))).astype(BF16)
    w_z = w_in[:, :, n_c:].astype(BF16)
    wq = w_uq.reshape(n, Q_LORA, MLA_HEADS, QK_NOPE + QK_ROPE)
    wq = jnp.pad(wq, ((0, 0), (0, 0), (0, 0), (0, QK_PAD - QK_NOPE - QK_ROPE)))
    wuq_t = wq.reshape(n, Q_LORA, MLA_HEADS * QK_PAD).transpose(0, 2, 1).astype(BF16)
    wkv = w_ukv.reshape(n, KV_LORA, MLA_HEADS, QK_NOPE + V_HEAD)
    wk = jnp.pad(wkv[..., :QK_NOPE], ((0, 0), (0, 0), (0, 0), (0, QK_PAD - QK_NOPE)))
    wk = wk.reshape(n, KV_LORA, MLA_HEADS * QK_PAD).astype(BF16)
    wv_t = wkv[..., QK_NOPE:].reshape(n, KV_LORA, MLA_HEADS * V_HEAD).transpose(0, 2, 1).astype(BF16)
    proj = (w_c, w_z, q_norm.reshape(n, 1, Q_LORA), kv_norm.reshape(n, 1, KV_LORA), wuq_t, wk, wv_t)
    return proj, w_out.astype(BF16)


def _mla_proj_call(x, mod, pre_g, layer, params, cn, sn, ct, st):
    bsz, seq, d = x.shape
    ts = SEQ_TILE
    nt = seq // ts
    half = QK_ROPE // 2
    j = layer // 2
    tile = pl.BlockSpec((1, ts, d), lambda b, s: (b, s, 0))
    return pl.pallas_call(
        _mla_proj_kernel,
        grid=(bsz, nt),
        in_specs=[tile, _mod_spec(mod, layer), _layer_spec(pre_g, layer)]
        + [_layer_spec(p, j) for p in params] + [
            pl.BlockSpec((1, ts, half), lambda b, s: (b, s, 0)),
            pl.BlockSpec((1, ts, half), lambda b, s: (b, s, 0)),
            pl.BlockSpec((1, half, ts), lambda b, s: (b, 0, s)),
            pl.BlockSpec((1, half, ts), lambda b, s: (b, 0, s)),
        ],
        out_specs=[
            pl.BlockSpec((1, MLA_HEADS, 1, 1, QK_PAD, ts),
                         lambda b, s: (b, 0, *_pair_index(s, nt), 0, 0)),
            pl.BlockSpec((1, MLA_HEADS, ts, QK_PAD), lambda b, s: (b, 0, s, 0)),
            pl.BlockSpec((1, MLA_HEADS, V_AUG, ts), lambda b, s: (b, 0, 0, s)),
            pl.BlockSpec((1, 1, 1, ts, d), lambda b, s: (b, *_pair_index(s, nt), 0, 0)),
        ],
        out_shape=[
            jax.ShapeDtypeStruct((bsz, MLA_HEADS, nt // 2, 2, QK_PAD, ts), BF16),
            jax.ShapeDtypeStruct((bsz, MLA_HEADS, seq, QK_PAD), BF16),
            jax.ShapeDtypeStruct((bsz, MLA_HEADS, V_AUG, seq), BF16),
            jax.ShapeDtypeStruct((bsz, nt // 2, 2, ts, d), BF16),
        ],
        scratch_shapes=[pltpu.VMEM((ts, d), BF16)],
        compiler_params=_params(2),
        name="mla_proj",
    )(x, mod, pre_g, *params, cn, sn, ct, st)


def _mla_attn_kernel(qt_ref, k_ref, vt_ref, sz_ref, o_ref,
                     s_lo0, s_hi0, s_lo1, s_hi1, m_lo0, m_hi0, m_lo1, m_hi1, p_lo, p_hi):
    tq = qt_ref.shape[-1]
    nt = k_ref.shape[2] // tq
    n_pair = nt // 2
    u_id = pl.program_id(2)
    heads = range(HEAD_GROUP)
    s_bufs = (((s_lo0, m_lo0), (s_hi0, m_hi0)), ((s_lo1, m_lo1), (s_hi1, m_hi1)))
    p_bufs = (p_lo, p_hi)

    def streams(slot, v_a, v_b, bufs_a, bufs_b):
        n_a = v_a + 1 if v_a is not None else 0
        n_b = v_b + 1 if v_b is not None else 0
        s_a, m_a = bufs_a
        s_b, m_bref = bufs_b
        p_scr = p_bufs[slot]
        key = lax.broadcasted_iota(jnp.int32, (tq, tq), 0)
        qry = lax.broadcasted_iota(jnp.int32, (tq, tq), 1)
        q_t = [qt_ref[0, hh, 0, slot] for hh in heads] if n_a else None
        m_b = [m_bref[hh] for hh in heads] if n_b else None
        m_part = [None] * HEAD_GROUP
        for j in range(max(n_a, n_b)):
            blk = slice(j * tq, (j + 1) * tq)
            if j < n_a:
                for hh in heads:
                    s = _dot(k_ref[0, hh, blk, :], q_t[hh])
                    if j == v_a:
                        s = jnp.where(key <= qry, s, NEG_BIG)
                    s_a[hh, blk, :] = s
                    pm = jnp.max(s.reshape(tq // MAX_PART, MAX_PART, tq), axis=0)
                    m_part[hh] = pm if m_part[hh] is None else jnp.maximum(m_part[hh], pm)
            if j < n_b:
                for hh in heads:
                    p = jnp.exp2(s_b[hh, blk, :] - m_b[hh]).astype(BF16)
                    p_scr[hh, blk, :] = p
            if n_b and j == n_b - 1:
                kend = n_b * tq
                outs = []
                for hh in heads:
                    o_aug = _dot(vt_ref[0, hh, :, 0:kend], p_scr[hh, 0:kend, :])
                    outs.append(o_aug[0:V_HEAD] / o_aug[V_HEAD:V_HEAD + 1])
                o = jnp.concatenate(outs, axis=0).T
                o_ref[0, 0, slot] = (o * sz_ref[0, 0, slot].astype(F32)).astype(BF16)
        if n_a:
            for hh in heads:
                m_a[hh] = jnp.max(m_part[hh], axis=0, keepdims=True)

    def step(u):
        for slot in range(2):
            tile = lambda w: w if slot == 0 else nt - 1 - w
            v_a = tile(u) if u < n_pair else None
            v_b = tile(u - 1) if u > 0 else None
            streams(slot, v_a, v_b, s_bufs[u % 2][slot], s_bufs[(u - 1) % 2][slot])

    for u in range(n_pair + 1):
        pl.when(u_id == u)(functools.partial(step, u))


def _pair_index(s, nt):
    return jnp.minimum(s, nt - 1 - s), s // (nt // 2)


def _mla_attn_call(qt, k, vt, sz):
    bsz, heads, n_pair, _, _, ts = qt.shape
    seq = 2 * n_pair * ts
    wide = HEAD_GROUP * V_HEAD
    lo_rows, hi_rows = n_pair * ts, seq
    prev = lambda u: jnp.maximum(u - 1, 0)
    io_spec = pl.BlockSpec((1, 1, 2, ts, wide), lambda b, p, u: (b, prev(u), 0, 0, p))
    s_shapes = [pltpu.VMEM((HEAD_GROUP, rows, ts), F32) for _ in range(2) for rows in (lo_rows, hi_rows)]
    m_shapes = [pltpu.VMEM((HEAD_GROUP, 1, ts), F32) for _ in range(4)]
    p_shapes = [pltpu.VMEM((HEAD_GROUP, rows, ts), BF16) for rows in (lo_rows, hi_rows)]
    return pl.pallas_call(
        _mla_attn_kernel,
        grid=(bsz, heads // HEAD_GROUP, n_pair + 1),
        in_specs=[
            pl.BlockSpec((1, HEAD_GROUP, 1, 2, QK_PAD, ts),
                         lambda b, p, u: (b, p, jnp.minimum(u, n_pair - 1), 0, 0, 0)),
            pl.BlockSpec((1, HEAD_GROUP, seq, QK_PAD), lambda b, p, u: (b, p, 0, 0)),
            pl.BlockSpec((1, HEAD_GROUP, V_AUG, seq), lambda b, p, u: (b, p, 0, 0)),
            io_spec,
        ],
        out_specs=io_spec,
        out_shape=jax.ShapeDtypeStruct((bsz, n_pair, 2, ts, heads * V_HEAD), BF16),
        scratch_shapes=s_shapes + m_shapes + p_shapes,
        compiler_params=_params(3),
        name="mla_attn",
    )(qt, k, vt, sz)


def _mla_out_kernel(x_ref, o_ref_in, mod_ref, post_g_ref, w_out_ref, out_ref):
    y = _dot(o_ref_in[0, 0, 0], w_out_ref[...])
    out_ref[0] = _postnorm_residual(x_ref[0], y, mod_ref, post_g_ref)


def _mla_out_call(x, o, mod, post_g, layer, w_out):
    bsz, seq, d = x.shape
    ts = SEQ_TILE
    nt = seq // ts
    tile = pl.BlockSpec((1, ts, d), lambda b, s: (b, s, 0))
    return pl.pallas_call(
        _mla_out_kernel,
        grid=(bsz, nt),
        in_specs=[
            tile,
            pl.BlockSpec((1, 1, 1, ts, d), lambda b, s: (b, *_pair_index(s, nt), 0, 0)),
            _mod_spec(mod, layer),
            _layer_spec(post_g, layer),
            _layer_spec(w_out, layer // 2),
        ],
        out_specs=tile,
        out_shape=jax.ShapeDtypeStruct(x.shape, F32),
        compiler_params=_params(2),
        name="mla_out",
    )(x, o, mod, post_g, w_out)


def kernel(x, c, positions, ada_w, ada_b, pre_g, post_g, ev_w_in, ev_conv_w, ev_conv_b, ev_ln_g, ev_ln_b, ev_lru_conv_w, ev_lru_conv_b, ev_lru_wa, ev_lru_ba, ev_lru_wx, ev_lru_bx, ev_lru_lam, ev_w_out, od_w_in, od_q_norm, od_kv_norm, od_w_uq, od_w_ukv, od_w_out):
    depth = ada_w.shape[0]
    bsz, seq, d = x.shape
    assert d == D_MODEL and seq % (2 * SEQ_TILE) == 0 and seq % EVEN_TILE == 0
    mod = _ada_call(c, ada_w, ada_b).reshape(depth, bsz, 3, d)
    cn, sn, ct, st = _rope_call(positions)
    pre_g = pre_g.reshape(depth, 1, d)
    post_g = post_g.reshape(depth, 1, d)
    even_params = _even_prep(ev_w_in, ev_conv_w, ev_conv_b, ev_ln_g, ev_ln_b, ev_lru_conv_w,
                             ev_lru_conv_b, ev_lru_wa, ev_lru_ba, ev_lru_wx, ev_lru_bx, ev_lru_lam,
                             ev_w_out)
    proj_params, od_w_out_b = _mla_prep(od_w_in, od_q_norm, od_kv_norm, od_w_uq, od_w_ukv, od_w_out)
    for layer in range(depth):
        if layer % 2 == 0:
            x = _even_call(x, mod, pre_g, post_g, layer, even_params)
        else:
            qt, k, vt, sz = _mla_proj_call(x, mod, pre_g, layer, proj_params, cn, sn, ct, st)
            o = _mla_attn_call(qt, k, vt, sz)
            x = _mla_out_call(x, o, mod, post_g, layer, od_w_out_b)
    return x
```

```python
import functools

import jax
import jax.numpy as jnp
from jax import lax
from jax.experimental import pallas as pl
from jax.experimental.pallas import tpu as pltpu

F32 = jnp.float32
BF16 = jnp.bfloat16

D_MODEL = 1024
CONV_KERNEL = 31
LRU_CONV_KERNEL = 4
LRU_HEADS = 16
LRU_HEAD_DIM = D_MODEL // LRU_HEADS
LRU_C = 8.0
MLA_HEADS = 16
QK_NOPE = 64
QK_ROPE = 32
V_HEAD = 64
Q_LORA = 256
KV_LORA = 256
ROPE_THETA = 10000.0
EPS = 1e-6

LANE = 128
SUBLANE = 8
MXU_DIM = 256

SEQ_TILE = 256
MLA_TILE = 2 * SEQ_TILE
EVEN_TILE = 512
N_SLAB = D_MODEL // LANE
CONV_HALO = 32
LRU_HALO = 8
CONV_TAPS_PAD = 32
GATE_GROUP = MXU_DIM
N_GATE_GROUP = D_MODEL // GATE_GROUP
CONV_ROWS = 64
QK_PAD = LANE
HEAD_GROUP = 4
MAX_PART = 32
V_AUG = V_HEAD + 16
LOG2_E = 1.4426950408889634
NEG_BIG = -1e30

VMEM_LIMIT_BYTES = 56 * 1024 * 1024


def _dot(a, b):
    return jnp.dot(a, b, preferred_element_type=F32)


def _dot_nt(a, b):
    return lax.dot_general(a, b, (((1,), (1,)), ((), ())), preferred_element_type=F32)


def _sigmoid(x):
    return 0.5 * jnp.tanh(0.5 * x) + 0.5


def _silu(x):
    h = 0.5 * x
    return h + h * jnp.tanh(h)


def _layer_spec(stacked, j):
    rest = stacked.shape[1:]
    return pl.BlockSpec((None,) + rest, lambda *_: (j,) + (0,) * len(rest),
                        pipeline_mode=pl.Buffered(1))


def _mod_spec(mod, layer):
    return pl.BlockSpec((None, 1) + mod.shape[2:], lambda b, *_: (layer, b, 0, 0))


def _params(n_grid):
    return pltpu.CompilerParams(
        dimension_semantics=("arbitrary",) * n_grid, vmem_limit_bytes=VMEM_LIMIT_BYTES)


def _ada_kernel(c_ref, w_ref, b_ref, o_ref):
    c = c_ref[...]
    o_ref[0] = jnp.dot(_silu(c), w_ref[0], preferred_element_type=F32,
                       precision=lax.Precision.HIGHEST) + b_ref[0]


def _ada_call(c, ada_w, ada_b):
    depth, d, d3 = ada_w.shape
    bsz = c.shape[0]
    n_col = d3 // d
    return pl.pallas_call(
        _ada_kernel,
        grid=(depth, n_col),
        in_specs=[
            pl.BlockSpec((bsz, d), lambda l, j: (0, 0)),
            pl.BlockSpec((1, d, d), lambda l, j: (l, 0, j)),
            pl.BlockSpec((1, 1, d), lambda l, j: (l, 0, j)),
        ],
        out_specs=pl.BlockSpec((1, bsz, d), lambda l, j: (l, 0, j)),
        out_shape=jax.ShapeDtypeStruct((depth, bsz, d3), F32),
        compiler_params=_params(2),
        name="ada_mod",
    )(c, ada_w, ada_b.reshape(depth, 1, d3))


def _rope_kernel(posrep_ref, invrep_ref, pos_ref, invcol_ref, cn_ref, sn_ref, ct_ref, st_ref):
    ang_n = posrep_ref[0].astype(F32) * invrep_ref[...]
    cn_ref[0] = jnp.cos(ang_n)
    sn_ref[0] = jnp.sin(ang_n)
    ang_t = invcol_ref[...] * pos_ref[0].astype(F32)
    ct_ref[0] = jnp.cos(ang_t)
    st_ref[0] = jnp.sin(ang_t)


def _rope_call(positions):
    bsz, seq = positions.shape
    half = QK_ROPE // 2
    rep = LANE // half
    inv = ROPE_THETA ** (-jnp.arange(0, QK_ROPE, 2, dtype=F32) / QK_ROPE)
    posrep = jnp.repeat(positions, half, axis=1).reshape(bsz, seq // rep, LANE)
    invrep = jnp.tile(inv, rep).reshape(1, LANE)
    n_spec = pl.BlockSpec((1, seq // rep, LANE), lambda b: (b, 0, 0))
    t_spec = pl.BlockSpec((1, half, seq), lambda b: (b, 0, 0))
    cn, sn, ct, st = pl.pallas_call(
        _rope_kernel,
        grid=(bsz,),
        in_specs=[
            n_spec,
            pl.BlockSpec((1, LANE), lambda b: (0, 0)),
            pl.BlockSpec((1, 1, seq), lambda b: (b, 0, 0)),
            pl.BlockSpec((half, 1), lambda b: (0, 0)),
        ],
        out_specs=[n_spec, n_spec, t_spec, t_spec],
        out_shape=[jax.ShapeDtypeStruct((bsz, seq // rep, LANE), F32)] * 2
        + [jax.ShapeDtypeStruct((bsz, half, seq), F32)] * 2,
        compiler_params=_params(1),
        name="rope_tables",
    )(posrep, invrep, positions.reshape(bsz, 1, seq), inv.reshape(half, 1))
    return cn.reshape(bsz, seq, half), sn.reshape(bsz, seq, half), ct, st


def _prenorm(x, mod_ref, pre_g_ref):
    ms = jnp.mean(x * x, axis=-1, keepdims=True)
    shift = mod_ref[0, 0:1, :]
    scale = mod_ref[0, 1:2, :]
    gmul = pre_g_ref[...] * (1.0 + scale)
    return x * lax.rsqrt(ms + EPS) * gmul + shift


def _postnorm_residual(x, y, mod_ref, post_g_ref):
    ms = jnp.mean(y * y, axis=-1, keepdims=True)
    gate = mod_ref[0, 2:3, :]
    return x + (gate * post_g_ref[...]) * (y * lax.rsqrt(ms + EPS))


def _scan_slab(a, b, carry):
    t = a.shape[0]
    groups = t // SUBLANE
    a = a.reshape(groups, SUBLANE, LANE)
    b = b.reshape(groups, SUBLANE, LANE)
    row = lax.broadcasted_iota(jnp.int32, a.shape, 1)
    d = 1
    while d < SUBLANE:
        ok = row >= d
        b = jnp.where(ok, a, 0.0) * pltpu.roll(b, d, axis=1) + b
        a = a * jnp.where(ok, pltpu.roll(a, d, axis=1), 1.0)
        d *= 2
    outs = []
    c = carry
    for v in range(groups):
        hv = b[v] + a[v] * c
        outs.append(hv)
        c = jnp.broadcast_to(hv[SUBLANE - 1:SUBLANE, :], (SUBLANE, LANE))
    return jnp.concatenate(outs, axis=0), c


def _even_kernel(x_ref, mod_ref, pre_g_ref, post_g_ref, w_vg_ref, w_rest_ref, cw_ref, cb_ref, lng_ref,
                 lnb_ref, lcw_ref, lcb_ref, wg_ref, ba_ref, bx_ref, lam_ref, w_out_ref, o_ref,
                 hbf, u0, u1, u2, ur, cbuf, cv, lbuf, xcf, xcb, a_s, b_s, hcar, act):
    ts = x_ref.shape[1]
    d = D_MODEL
    gw = GATE_GROUP
    spg = gw // LANE

    def rest(kind, c):
        lo = kind * gw + (c % spg) * LANE
        return ur[c // spg, :, lo:lo + LANE]

    @pl.when(pl.program_id(1) == 0)
    def _():
        cbuf[:, 0:CONV_HALO, :] = jnp.zeros((N_SLAB, CONV_HALO, LANE), F32)
        lbuf[:, 0:LRU_HALO, :] = jnp.zeros((N_SLAB, LRU_HALO, LANE), F32)
        hcar[...] = jnp.zeros(hcar.shape, F32)

    hbf[...] = _prenorm(x_ref[0], mod_ref, pre_g_ref).astype(BF16)

    u0[...] = _dot(hbf[...], w_vg_ref[:, 0:d])
    u1[...] = _dot(hbf[...], w_vg_ref[:, d:2 * d])
    u2[...] = _dot(hbf[...], w_vg_ref[:, 2 * d:3 * d])
    for g in range(N_GATE_GROUP):
        ur[g] = _dot(hbf[...], w_rest_ref[g])
    for c in range(N_SLAB):
        sl = slice(c * LANE, (c + 1) * LANE)
        cbuf[c, CONV_HALO:CONV_HALO + ts, :] = u0[:, sl] * _sigmoid(u1[:, sl])

    first = CONV_HALO - (CONV_KERNEL - 1)
    tap_groups = [[j for j in range(CONV_KERNEL) if (first + j) % SUBLANE == r] for r in range(SUBLANE)]

    def conv_slab(c, carry):
        w = cw_ref[c]
        for r0 in range(0, ts, CONV_ROWS):
            acc = jnp.broadcast_to(cb_ref[c], (CONV_ROWS, LANE))
            for taps in tap_groups:
                start = first + taps[0] + r0
                win = cbuf[c, start:start + CONV_ROWS + SUBLANE * (len(taps) - 1), :]
                for q, j in enumerate(taps):
                    acc = acc + win[SUBLANE * q:SUBLANE * q + CONV_ROWS] * w[j:j + 1, :]
            cv[c, r0:r0 + CONV_ROWS, :] = acc
        return carry

    lax.fori_loop(0, N_SLAB, conv_slab, 0)

    tot = cv[0]
    for c in range(1, N_SLAB):
        tot = tot + cv[c]
    mu = jnp.broadcast_to(jnp.sum(tot, axis=-1, keepdims=True) * (1.0 / d), (ts, LANE))
    sq = jnp.zeros((ts, LANE), F32)
    for c in range(N_SLAB):
        dc = cv[c] - mu
        sq = sq + dc * dc
    rstd = lax.rsqrt(jnp.sum(sq, axis=-1, keepdims=True) * (1.0 / d) + EPS)
    rstd = jnp.broadcast_to(rstd, (ts, LANE))
    for c in range(N_SLAB):
        sl = slice(c * LANE, (c + 1) * LANE)
        yn = (cv[c] - mu) * rstd * lng_ref[:, sl] + lnb_ref[:, sl]
        act[:, sl] = (_silu(yn) * _silu(rest(0, c))).astype(BF16)

    for c in range(N_SLAB):
        sl = slice(c * LANE, (c + 1) * LANE)
        lbuf[c, LRU_HALO:LRU_HALO + ts, :] = u2[:, sl]
        acc = jnp.broadcast_to(lcb_ref[:, sl], (ts, LANE))
        for j in range(LRU_CONV_KERNEL):
            start = LRU_HALO - (LRU_CONV_KERNEL - 1) + j
            acc = acc + lbuf[c, start:start + ts, :] * lcw_ref[j:j + 1, sl]
        xcf[:, sl] = acc
        xcb[:, sl] = acc.astype(BF16)

    nl = -lam_ref[...]
    softplus = jnp.maximum(nl, 0.0) + jnp.log1p(jnp.exp(-jnp.abs(nl)))
    for g in range(N_GATE_GROUP):
        sl = slice(g * gw, (g + 1) * gw)
        rg = _dot(xcb[:, sl], wg_ref[g])
        r = _sigmoid(rg[:, 0:gw] + ba_ref[:, sl])
        i = _sigmoid(rg[:, gw:2 * gw] + bx_ref[:, sl])
        a = jnp.exp((-LRU_C) * r * softplus[:, sl])
        a_s[:, sl] = a
        b_s[:, sl] = jnp.sqrt(1.0 - a * a) * i * xcf[:, sl]

    for c in range(N_SLAB):
        sl = slice(c * LANE, (c + 1) * LANE)
        h, c_new = _scan_slab(a_s[:, sl], b_s[:, sl], hcar[:, sl])
        hcar[:, sl] = c_new
        act[:, d + c * LANE:d + (c + 1) * LANE] = (h * _silu(rest(1, c))).astype(BF16)

    cbuf[:, 0:CONV_HALO, :] = cbuf[:, ts:ts + CONV_HALO, :]
    lbuf[:, 0:LRU_HALO, :] = lbuf[:, ts:ts + LRU_HALO, :]

    y = _dot(act[...], w_out_ref[...])
    o_ref[0] = _postnorm_residual(x_ref[0], y, mod_ref, post_g_ref)


def _even_prep(w_in, conv_w, conv_b, ln_g, ln_b, lcw, lcb, wa, ba, wx, bx, lam, w_out):
    n, d = w_in.shape[0], D_MODEL
    ng, gw = N_GATE_GROUP, GATE_GROUP
    row = lambda v: v.reshape(n, 1, d)
    cw = jnp.pad(conv_w, ((0, 0), (0, CONV_TAPS_PAD - CONV_KERNEL), (0, 0)))
    cw = cw.reshape(n, CONV_TAPS_PAD, N_SLAB, LANE).transpose(0, 2, 1, 3)
    cb = conv_b.reshape(n, N_SLAB, 1, LANE)
    hpg = gw // LRU_HEAD_DIM
    eye = jnp.eye(hpg, dtype=F32)

    def blockdiag(w):
        wg = w.reshape(n, ng, hpg, LRU_HEAD_DIM, LRU_HEAD_DIM)
        return jnp.einsum('lghij,hk->lghikj', wg, eye).reshape(n, ng, gw, gw)

    wg = jnp.concatenate([blockdiag(wa), blockdiag(wx)], axis=-1).astype(BF16)
    w_vg = jnp.concatenate([w_in[:, :, 0:2 * d], w_in[:, :, 3 * d:4 * d]], axis=2).astype(BF16)
    grouped = lambda lo: w_in[:, :, lo * d:(lo + 1) * d].reshape(n, d, ng, gw)
    w_rest = jnp.concatenate([grouped(2), grouped(4)], axis=3)
    w_rest = w_rest.transpose(0, 2, 1, 3).astype(BF16)
    return (w_vg, w_rest, cw, cb, row(ln_g), row(ln_b), lcw, row(lcb), wg,
            row(ba), row(bx), row(lam), w_out.astype(BF16))


def _even_call(x, mod, pre_g, post_g, layer, params):
    bsz, seq, d = x.shape
    ts = EVEN_TILE
    ng, gw = N_GATE_GROUP, GATE_GROUP
    j = layer // 2
    tile = pl.BlockSpec((1, ts, d), lambda b, s: (b, s, 0))
    scratch = [
        pltpu.VMEM((ts, d), BF16),
        pltpu.VMEM((ts, d), F32),
        pltpu.VMEM((ts, d), F32),
        pltpu.VMEM((ts, d), F32),
        pltpu.VMEM((ng, ts, 2 * gw), F32),
        pltpu.VMEM((N_SLAB, CONV_HALO + ts, LANE), F32),
        pltpu.VMEM((N_SLAB, ts, LANE), F32),
        pltpu.VMEM((N_SLAB, LRU_HALO + ts, LANE), F32),
        pltpu.VMEM((ts, d), F32),
        pltpu.VMEM((ts, d), BF16),
        pltpu.VMEM((ts, d), F32),
        pltpu.VMEM((ts, d), F32),
        pltpu.VMEM((SUBLANE, d), F32),
        pltpu.VMEM((ts, 2 * d), BF16),
    ]
    return pl.pallas_call(
        _even_kernel,
        grid=(bsz, seq // ts),
        in_specs=[tile, _mod_spec(mod, layer), _layer_spec(pre_g, layer), _layer_spec(post_g, layer)]
        + [_layer_spec(p, j) for p in params],
        out_specs=tile,
        out_shape=jax.ShapeDtypeStruct(x.shape, F32),
        scratch_shapes=scratch,
        compiler_params=_params(2),
        name="even_layer",
    )(x, mod, pre_g, post_g, *params)


def _mla_proj_kernel(x_ref, mod_ref, pre_g_ref, w_c_ref, w_z_ref, qn_ref, kvn_ref,
                     wuq_t_ref, wk_ref, wv_t_ref, cn_ref, sn_ref, ct_ref, st_ref,
                     qt_ref, k_ref, vt_ref, sz_ref, hbf):
    ts = x_ref.shape[1]
    half = QK_ROPE // 2
    hbf[...] = _prenorm(x_ref[0], mod_ref, pre_g_ref).astype(BF16)

    uc = _dot(hbf[...], w_c_ref[...])
    sz_ref[0] = _silu(_dot(hbf[...], w_z_ref[...])).astype(BF16)

    def rms(v, g_ref):
        ms = jnp.mean(v * v, axis=-1, keepdims=True)
        return (v * lax.rsqrt(ms + EPS) * g_ref[...]).astype(BF16)

    cqn = rms(uc[:, 0:Q_LORA], qn_ref)
    ckvn = rms(uc[:, Q_LORA:Q_LORA + KV_LORA], kvn_ref)
    kr = uc[:, Q_LORA + KV_LORA:Q_LORA + KV_LORA + QK_ROPE]

    q_t = _dot_nt(wuq_t_ref[...], cqn)
    c_t = ct_ref[0]
    s_t = st_ref[0]
    scale = (QK_NOPE + QK_ROPE) ** -0.5 * LOG2_E
    for hd in range(MLA_HEADS):
        base = hd * QK_PAD
        x1 = q_t[base + QK_NOPE:base + QK_NOPE + half]
        x2 = q_t[base + QK_NOPE + half:base + QK_NOPE + QK_ROPE]
        blk = jnp.concatenate([
            q_t[base:base + QK_NOPE],
            x1 * c_t - x2 * s_t,
            x1 * s_t + x2 * c_t,
            q_t[base + QK_NOPE + QK_ROPE:base + QK_PAD],
        ], axis=0)
        blk = (blk * scale).astype(BF16)
        for slot in range(ts // SEQ_TILE):
            qt_ref[0, hd, 0, slot] = blk[:, slot * SEQ_TILE:(slot + 1) * SEQ_TILE]

    kv_k = _dot(ckvn, wk_ref[...])
    c_n = cn_ref[0]
    s_n = sn_ref[0]
    k1 = kr[:, 0:half]
    k2 = kr[:, half:QK_ROPE]
    kr128 = jnp.concatenate([
        jnp.zeros((ts, QK_NOPE), F32),
        k1 * c_n - k2 * s_n,
        k1 * s_n + k2 * c_n,
        jnp.zeros((ts, QK_PAD - QK_NOPE - QK_ROPE), F32),
    ], axis=-1)
    for hd in range(MLA_HEADS):
        k_ref[0, hd] = (kv_k[:, hd * QK_PAD:(hd + 1) * QK_PAD] + kr128).astype(BF16)

    v_t = _dot_nt(wv_t_ref[...], ckvn)
    for hd in range(MLA_HEADS):
        vt_ref[0, hd, 0:V_HEAD, :] = v_t[hd * V_HEAD:(hd + 1) * V_HEAD].astype(BF16)
        row = lax.broadcasted_iota(jnp.int32, (V_AUG - V_HEAD, ts), 0)
        vt_ref[0, hd, V_HEAD:V_AUG, :] = jnp.where(row == 0, 1.0, 0.0).astype(BF16)


def _mla_prep(w_in, q_norm, kv_norm, w_uq, w_ukv, w_out):
    n = w_in.shape[0]
    n_c = Q_LORA + KV_LORA + QK_ROPE
    w_c = jnp.pad(w_in[:, :, :n_c], ((0, 0), (0, 0), (0, (-n_c) % LANE))).astype(BF16)
    w_z = w_in[:, :, n_c:].astype(BF16)
    wq = w_uq.reshape(n, Q_LORA, MLA_HEADS, QK_NOPE + QK_ROPE)
    wq = jnp.pad(wq, ((0, 0), (0, 0), (0, 0), (0, QK_PAD - QK_NOPE - QK_ROPE)))
    wuq_t = wq.reshape(n, Q_LORA, MLA_HEADS * QK_PAD).transpose(0, 2, 1).astype(BF16)
    wkv = w_ukv.reshape(n, KV_LORA, MLA_HEADS, QK_NOPE + V_HEAD)
    wk = jnp.pad(wkv[..., :QK_NOPE], ((0, 0), (0, 0), (0, 0), (0, QK_PAD - QK_NOPE)))
    wk = wk.reshape(n, KV_LORA, MLA_HEADS * QK_PAD).astype(BF16)
    wv_t = wkv[..., QK_NOPE:].reshape(n, KV_LORA, MLA_HEADS * V_HEAD).transpose(0, 2, 1).astype(BF16)
    proj = (w_c, w_z, q_norm.reshape(n, 1, Q_LORA), kv_norm.reshape(n, 1, KV_LORA), wuq_t, wk, wv_t)
    return proj, w_out.astype(BF16)


def _mla_proj_call(x, mod, pre_g, layer, params, cn, sn, ct, st):
    bsz, seq, d = x.shape
    ts = MLA_TILE
    nt = seq // ts
    half = QK_ROPE // 2
    j = layer // 2
    tile = pl.BlockSpec((1, ts, d), lambda b, s: (b, s, 0))
    return pl.pallas_call(
        _mla_proj_kernel,
        grid=(bsz, nt),
        in_specs=[tile, _mod_spec(mod, layer), _layer_spec(pre_g, layer)]
        + [_layer_spec(p, j) for p in params] + [
            pl.BlockSpec((1, ts, half), lambda b, s: (b, s, 0)),
            pl.BlockSpec((1, ts, half), lambda b, s: (b, s, 0)),
            pl.BlockSpec((1, half, ts), lambda b, s: (b, 0, s)),
            pl.BlockSpec((1, half, ts), lambda b, s: (b, 0, s)),
        ],
        out_specs=[
            pl.BlockSpec((1, MLA_HEADS, 1, ts // SEQ_TILE, QK_PAD, SEQ_TILE),
                         lambda b, s: (b, 0, s, 0, 0, 0)),
            pl.BlockSpec((1, MLA_HEADS, ts, QK_PAD), lambda b, s: (b, 0, s, 0)),
            pl.BlockSpec((1, MLA_HEADS, V_AUG, ts), lambda b, s: (b, 0, 0, s)),
            tile,
        ],
        out_shape=[
            jax.ShapeDtypeStruct((bsz, MLA_HEADS, nt, ts // SEQ_TILE, QK_PAD, SEQ_TILE), BF16),
            jax.ShapeDtypeStruct((bsz, MLA_HEADS, seq, QK_PAD), BF16),
            jax.ShapeDtypeStruct((bsz, MLA_HEADS, V_AUG, seq), BF16),
            jax.ShapeDtypeStruct((bsz, seq, d), BF16),
        ],
        scratch_shapes=[pltpu.VMEM((ts, d), BF16)],
        compiler_params=_params(2),
        name="mla_proj",
    )(x, mod, pre_g, *params, cn, sn, ct, st)


def _mla_attn_kernel(qt_ref, k_ref, vt_ref, sz_ref, o_ref,
                     s_lo0, s_hi0, s_lo1, s_hi1, m_lo0, m_hi0, m_lo1, m_hi1, p_lo, p_hi):
    tq = qt_ref.shape[-1]
    nt = k_ref.shape[2] // tq
    n_pair = nt // 2
    u_id = pl.program_id(2)
    heads = range(HEAD_GROUP)
    s_bufs = (((s_lo0, m_lo0), (s_hi0, m_hi0)), ((s_lo1, m_lo1), (s_hi1, m_hi1)))
    p_bufs = (p_lo, p_hi)

    def streams(slot, v_a, v_b, bufs_a, bufs_b):
        n_a = v_a + 1 if v_a is not None else 0
        n_b = v_b + 1 if v_b is not None else 0
        s_a, m_a = bufs_a
        s_b, m_bref = bufs_b
        p_scr = p_bufs[slot]
        key = lax.broadcasted_iota(jnp.int32, (tq, tq), 0)
        qry = lax.broadcasted_iota(jnp.int32, (tq, tq), 1)
        q_t = [qt_ref[0, hh, 0, slot] for hh in heads] if n_a else None
        m_b = [m_bref[hh] for hh in heads] if n_b else None
        m_part = [None] * HEAD_GROUP
        for j in range(max(n_a, n_b)):
            blk = slice(j * tq, (j + 1) * tq)
            if j < n_a:
                for hh in heads:
                    s = _dot(k_ref[0, hh, blk, :], q_t[hh])
                    if j == v_a:
                        s = jnp.where(key <= qry, s, NEG_BIG)
                    s_a[hh, blk, :] = s
                    pm = jnp.max(s.reshape(tq // MAX_PART, MAX_PART, tq), axis=0)
                    m_part[hh] = pm if m_part[hh] is None else jnp.maximum(m_part[hh], pm)
            if j < n_b:
                for hh in heads:
                    p = jnp.exp2(s_b[hh, blk, :] - m_b[hh]).astype(BF16)
                    p_scr[hh, blk, :] = p
            if n_b and j == n_b - 1:
                kend = n_b * tq
                outs = []
                for hh in heads:
                    o_aug = _dot(vt_ref[0, hh, :, 0:kend], p_scr[hh, 0:kend, :])
                    outs.append(o_aug[0:V_HEAD] / o_aug[V_HEAD:V_HEAD + 1])
                o = jnp.concatenate(outs, axis=0).T
                rows = slice(slot * tq, (slot + 1) * tq)
                o_ref[0, rows, :] = (o * sz_ref[0, rows, :].astype(F32)).astype(BF16)
        if n_a:
            for hh in heads:
                m_a[hh] = jnp.max(m_part[hh], axis=0, keepdims=True)

    def step(u):
        for slot in range(2):
            tile = lambda w: 2 * w + slot
            v_a = tile(u) if u < n_pair else None
            v_b = tile(u - 1) if u > 0 else None
            streams(slot, v_a, v_b, s_bufs[u % 2][slot], s_bufs[(u - 1) % 2][slot])

    for u in range(n_pair + 1):
        pl.when(u_id == u)(functools.partial(step, u))


def _mla_attn_call(qt, k, vt, sz):
    bsz, heads, n_pair, _, _, ts = qt.shape
    seq = 2 * n_pair * ts
    wide = HEAD_GROUP * V_HEAD
    lo_rows, hi_rows = seq - ts, seq
    prev = lambda u: jnp.maximum(u - 1, 0)
    io_spec = pl.BlockSpec((1, 2 * ts, wide), lambda b, p, u: (b, prev(u), p))
    s_shapes = [pltpu.VMEM((HEAD_GROUP, rows, ts), F32) for _ in range(2) for rows in (lo_rows, hi_rows)]
    m_shapes = [pltpu.VMEM((HEAD_GROUP, 1, ts), F32) for _ in range(4)]
    p_shapes = [pltpu.VMEM((HEAD_GROUP, rows, ts), BF16) for rows in (lo_rows, hi_rows)]
    return pl.pallas_call(
        _mla_attn_kernel,
        grid=(bsz, heads // HEAD_GROUP, n_pair + 1),
        in_specs=[
            pl.BlockSpec((1, HEAD_GROUP, 1, 2, QK_PAD, ts),
                         lambda b, p, u: (b, p, jnp.minimum(u, n_pair - 1), 0, 0, 0)),
            pl.BlockSpec((1, HEAD_GROUP, seq, QK_PAD), lambda b, p, u: (b, p, 0, 0)),
            pl.BlockSpec((1, HEAD_GROUP, V_AUG, seq), lambda b, p, u: (b, p, 0, 0)),
            io_spec,
        ],
        out_specs=io_spec,
        out_shape=jax.ShapeDtypeStruct((bsz, seq, heads * V_HEAD), BF16),
        scratch_shapes=s_shapes + m_shapes + p_shapes,
        compiler_params=_params(3),
        name="mla_attn",
    )(qt, k, vt, sz)


def _mla_out_kernel(x_ref, o_ref_in, mod_ref, post_g_ref, w_out_ref, out_ref):
    y = _dot(o_ref_in[0], w_out_ref[...])
    out_ref[0] = _postnorm_residual(x_ref[0], y, mod_ref, post_g_ref)


def _mla_out_call(x, o, mod, post_g, layer, w_out):
    bsz, seq, d = x.shape
    ts = MLA_TILE
    nt = seq // ts
    tile = pl.BlockSpec((1, ts, d), lambda b, s: (b, s, 0))
    return pl.pallas_call(
        _mla_out_kernel,
        grid=(bsz, nt),
        in_specs=[
            tile,
            tile,
            _mod_spec(mod, layer),
            _layer_spec(post_g, layer),
            _layer_spec(w_out, layer // 2),
        ],
        out_specs=tile,
        out_shape=jax.ShapeDtypeStruct(x.shape, F32),
        compiler_params=_params(2),
        name="mla_out",
    )(x, o, mod, post_g, w_out)


def kernel(x, c, positions, ada_w, ada_b, pre_g, post_g, ev_w_in, ev_conv_w, ev_conv_b, ev_ln_g, ev_ln_b, ev_lru_conv_w, ev_lru_conv_b, ev_lru_wa, ev_lru_ba, ev_lru_wx, ev_lru_bx, ev_lru_lam, ev_w_out, od_w_in, od_q_norm, od_kv_norm, od_w_uq, od_w_ukv, od_w_out):
    depth = ada_w.shape[0]
    bsz, seq, d = x.shape
    assert d == D_MODEL and seq % MLA_TILE == 0 and seq % EVEN_TILE == 0
    mod = _ada_call(c, ada_w, ada_b).reshape(depth, bsz, 3, d)
    cn, sn, ct, st = _rope_call(positions)
    pre_g = pre_g.reshape(depth, 1, d)
    post_g = post_g.reshape(depth, 1, d)
    even_params = _even_prep(ev_w_in, ev_conv_w, ev_conv_b, ev_ln_g, ev_ln_b, ev_lru_conv_w,
                             ev_lru_conv_b, ev_lru_wa, ev_lru_ba, ev_lru_wx, ev_lru_bx, ev_lru_lam,
                             ev_w_out)
    proj_params, od_w_out_b = _mla_prep(od_w_in, od_q_norm, od_kv_norm, od_w_uq, od_w_ukv, od_w_out)
    for layer in range(depth):
        if layer % 2 == 0:
            x = _even_call(x, mod, pre_g, post_g, layer, even_params)
        else:
            qt, k, vt, sz = _mla_proj_call(x, mod, pre_g, layer, proj_params, cn, sn, ct, st)
            o = _mla_attn_call(qt, k, vt, sz)
            x = _mla_out_call(x, o, mod, post_g, layer, od_w_out_b)
    return x
```

```python
import functools

import jax
import jax.numpy as jnp
from jax import lax
from jax.experimental import pallas as pl
from jax.experimental.pallas import tpu as pltpu

F32 = jnp.float32
BF16 = jnp.bfloat16

D_MODEL = 1024
CONV_KERNEL = 31
LRU_CONV_KERNEL = 4
LRU_HEADS = 16
LRU_HEAD_DIM = D_MODEL // LRU_HEADS
LRU_C = 8.0
MLA_HEADS = 16
QK_NOPE = 64
QK_ROPE = 32
V_HEAD = 64
Q_LORA = 256
KV_LORA = 256
ROPE_THETA = 10000.0
EPS = 1e-6

LANE = 128
SUBLANE = 8
MXU_DIM = 256

SEQ_TILE = 256
MLA_TILE = 2 * SEQ_TILE
EVEN_TILE = 512
OUT_TILE = 1024
N_SLAB = D_MODEL // LANE
CONV_HALO = 32
LRU_HALO = 8
CONV_TAPS_PAD = 32
GATE_GROUP = MXU_DIM
N_GATE_GROUP = D_MODEL // GATE_GROUP
CONV_ROWS = 64
QK_PAD = LANE
HEAD_GROUP = 4
MAX_PART = 32
V_AUG = V_HEAD + 16
LOG2_E = 1.4426950408889634
NEG_BIG = -1e30

VMEM_LIMIT_BYTES = 56 * 1024 * 1024


def _dot(a, b):
    return jnp.dot(a, b, preferred_element_type=F32)


def _dot_nt(a, b):
    return lax.dot_general(a, b, (((1,), (1,)), ((), ())), preferred_element_type=F32)


def _sigmoid(x):
    return 0.5 * jnp.tanh(0.5 * x) + 0.5


def _silu(x):
    h = 0.5 * x
    return h + h * jnp.tanh(h)


def _layer_spec(stacked, j):
    rest = stacked.shape[1:]
    return pl.BlockSpec((None,) + rest, lambda *_: (j,) + (0,) * len(rest),
                        pipeline_mode=pl.Buffered(1))


def _mod_spec(mod, layer):
    return pl.BlockSpec((None, 1) + mod.shape[2:], lambda b, *_: (layer, b, 0, 0))


def _params(n_grid):
    return pltpu.CompilerParams(
        dimension_semantics=("arbitrary",) * n_grid, vmem_limit_bytes=VMEM_LIMIT_BYTES)


def _ada_kernel(c_ref, w_ref, b_ref, o_ref):
    c = c_ref[...]
    o_ref[0] = jnp.dot(_silu(c), w_ref[0], preferred_element_type=F32,
                       precision=lax.Precision.HIGHEST) + b_ref[0]


def _ada_call(c, ada_w, ada_b):
    depth, d, d3 = ada_w.shape
    bsz = c.shape[0]
    n_col = d3 // d
    return pl.pallas_call(
        _ada_kernel,
        grid=(depth, n_col),
        in_specs=[
            pl.BlockSpec((bsz, d), lambda l, j: (0, 0)),
            pl.BlockSpec((1, d, d), lambda l, j: (l, 0, j)),
            pl.BlockSpec((1, 1, d), lambda l, j: (l, 0, j)),
        ],
        out_specs=pl.BlockSpec((1, bsz, d), lambda l, j: (l, 0, j)),
        out_shape=jax.ShapeDtypeStruct((depth, bsz, d3), F32),
        compiler_params=_params(2),
        name="ada_mod",
    )(c, ada_w, ada_b.reshape(depth, 1, d3))


def _rope_kernel(posrep_ref, invrep_ref, pos_ref, invcol_ref, cn_ref, sn_ref, ct_ref, st_ref):
    ang_n = posrep_ref[0].astype(F32) * invrep_ref[...]
    cn_ref[0] = jnp.cos(ang_n)
    sn_ref[0] = jnp.sin(ang_n)
    ang_t = invcol_ref[...] * pos_ref[0].astype(F32)
    ct_ref[0] = jnp.cos(ang_t)
    st_ref[0] = jnp.sin(ang_t)


def _rope_call(positions):
    bsz, seq = positions.shape
    half = QK_ROPE // 2
    rep = LANE // half
    inv = ROPE_THETA ** (-jnp.arange(0, QK_ROPE, 2, dtype=F32) / QK_ROPE)
    posrep = jnp.repeat(positions, half, axis=1).reshape(bsz, seq // rep, LANE)
    invrep = jnp.tile(inv, rep).reshape(1, LANE)
    n_spec = pl.BlockSpec((1, seq // rep, LANE), lambda b: (b, 0, 0))
    t_spec = pl.BlockSpec((1, half, seq), lambda b: (b, 0, 0))
    cn, sn, ct, st = pl.pallas_call(
        _rope_kernel,
        grid=(bsz,),
        in_specs=[
            n_spec,
            pl.BlockSpec((1, LANE), lambda b: (0, 0)),
            pl.BlockSpec((1, 1, seq), lambda b: (b, 0, 0)),
            pl.BlockSpec((half, 1), lambda b: (0, 0)),
        ],
        out_specs=[n_spec, n_spec, t_spec, t_spec],
        out_shape=[jax.ShapeDtypeStruct((bsz, seq // rep, LANE), F32)] * 2
        + [jax.ShapeDtypeStruct((bsz, half, seq), F32)] * 2,
        compiler_params=_params(1),
        name="rope_tables",
    )(posrep, invrep, positions.reshape(bsz, 1, seq), inv.reshape(half, 1))
    return cn.reshape(bsz, seq, half), sn.reshape(bsz, seq, half), ct, st


def _prenorm(x, mod_ref, pre_g_ref):
    ms = jnp.mean(x * x, axis=-1, keepdims=True)
    shift = mod_ref[0, 0:1, :]
    scale = mod_ref[0, 1:2, :]
    gmul = pre_g_ref[...] * (1.0 + scale)
    return x * lax.rsqrt(ms + EPS) * gmul + shift


def _postnorm_residual(x, y, mod_ref, post_g_ref):
    ms = jnp.mean(y * y, axis=-1, keepdims=True)
    gate = mod_ref[0, 2:3, :]
    return x + (gate * post_g_ref[...]) * (y * lax.rsqrt(ms + EPS))


def _scan_slab(a, b, carry):
    t = a.shape[0]
    groups = t // SUBLANE
    a = a.reshape(groups, SUBLANE, LANE)
    b = b.reshape(groups, SUBLANE, LANE)
    row = lax.broadcasted_iota(jnp.int32, a.shape, 1)
    d = 1
    while d < SUBLANE:
        ok = row >= d
        b = jnp.where(ok, a, 0.0) * pltpu.roll(b, d, axis=1) + b
        a = a * jnp.where(ok, pltpu.roll(a, d, axis=1), 1.0)
        d *= 2
    outs = []
    c = carry
    for v in range(groups):
        hv = b[v] + a[v] * c
        outs.append(hv)
        c = jnp.broadcast_to(hv[SUBLANE - 1:SUBLANE, :], (SUBLANE, LANE))
    return jnp.concatenate(outs, axis=0), c


def _even_kernel(x_ref, mod_ref, pre_g_ref, post_g_ref, w_vg_ref, w_rest_ref, cw_ref, cb_ref, lng_ref,
                 lnb_ref, lcw_ref, lcb_ref, wg_ref, ba_ref, bx_ref, lam_ref, w_out_ref, o_ref,
                 hbf, u0, u1, u2, ur, cbuf, cv, lbuf, xcf, xcb, a_s, b_s, hcar, act):
    ts = x_ref.shape[1]
    d = D_MODEL
    gw = GATE_GROUP
    spg = gw // LANE

    def rest(kind, c):
        lo = kind * gw + (c % spg) * LANE
        return ur[c // spg, :, lo:lo + LANE]

    @pl.when(pl.program_id(1) == 0)
    def _():
        cbuf[:, 0:CONV_HALO, :] = jnp.zeros((N_SLAB, CONV_HALO, LANE), F32)
        lbuf[:, 0:LRU_HALO, :] = jnp.zeros((N_SLAB, LRU_HALO, LANE), F32)
        hcar[...] = jnp.zeros(hcar.shape, F32)

    hbf[...] = _prenorm(x_ref[0], mod_ref, pre_g_ref).astype(BF16)

    u0[...] = _dot(hbf[...], w_vg_ref[:, 0:d])
    u1[...] = _dot(hbf[...], w_vg_ref[:, d:2 * d])
    u2[...] = _dot(hbf[...], w_vg_ref[:, 2 * d:3 * d])
    for g in range(N_GATE_GROUP):
        ur[g] = _dot(hbf[...], w_rest_ref[g])
    for c in range(N_SLAB):
        sl = slice(c * LANE, (c + 1) * LANE)
        cbuf[c, CONV_HALO:CONV_HALO + ts, :] = u0[:, sl] * _sigmoid(u1[:, sl])

    first = CONV_HALO - (CONV_KERNEL - 1)
    tap_groups = [[j for j in range(CONV_KERNEL) if (first + j) % SUBLANE == r] for r in range(SUBLANE)]

    def conv_slab(c, carry):
        w = cw_ref[c]
        for r0 in range(0, ts, CONV_ROWS):
            acc = jnp.broadcast_to(cb_ref[c], (CONV_ROWS, LANE))
            for taps in tap_groups:
                start = first + taps[0] + r0
                win = cbuf[c, start:start + CONV_ROWS + SUBLANE * (len(taps) - 1), :]
                for q, j in enumerate(taps):
                    acc = acc + win[SUBLANE * q:SUBLANE * q + CONV_ROWS] * w[j:j + 1, :]
            cv[c, r0:r0 + CONV_ROWS, :] = acc
        return carry

    lax.fori_loop(0, N_SLAB, conv_slab, 0)

    tot = cv[0]
    for c in range(1, N_SLAB):
        tot = tot + cv[c]
    mu = jnp.broadcast_to(jnp.sum(tot, axis=-1, keepdims=True) * (1.0 / d), (ts, LANE))
    sq = jnp.zeros((ts, LANE), F32)
    for c in range(N_SLAB):
        dc = cv[c] - mu
        sq = sq + dc * dc
    rstd = lax.rsqrt(jnp.sum(sq, axis=-1, keepdims=True) * (1.0 / d) + EPS)
    rstd = jnp.broadcast_to(rstd, (ts, LANE))
    for c in range(N_SLAB):
        sl = slice(c * LANE, (c + 1) * LANE)
        yn = (cv[c] - mu) * rstd * lng_ref[:, sl] + lnb_ref[:, sl]
        act[:, sl] = (_silu(yn) * _silu(rest(0, c))).astype(BF16)

    for c in range(N_SLAB):
        sl = slice(c * LANE, (c + 1) * LANE)
        lbuf[c, LRU_HALO:LRU_HALO + ts, :] = u2[:, sl]
        acc = jnp.broadcast_to(lcb_ref[:, sl], (ts, LANE))
        for j in range(LRU_CONV_KERNEL):
            start = LRU_HALO - (LRU_CONV_KERNEL - 1) + j
            acc = acc + lbuf[c, start:start + ts, :] * lcw_ref[j:j + 1, sl]
        xcf[:, sl] = acc
        xcb[:, sl] = acc.astype(BF16)

    nl = -lam_ref[...]
    softplus = jnp.maximum(nl, 0.0) + jnp.log1p(jnp.exp(-jnp.abs(nl)))
    for g in range(N_GATE_GROUP):
        sl = slice(g * gw, (g + 1) * gw)
        rg = _dot(xcb[:, sl], wg_ref[g])
        r = _sigmoid(rg[:, 0:gw] + ba_ref[:, sl])
        i = _sigmoid(rg[:, gw:2 * gw] + bx_ref[:, sl])
        a = jnp.exp((-LRU_C) * r * softplus[:, sl])
        a_s[:, sl] = a
        b_s[:, sl] = jnp.sqrt(1.0 - a * a) * i * xcf[:, sl]

    for c in range(N_SLAB):
        sl = slice(c * LANE, (c + 1) * LANE)
        h, c_new = _scan_slab(a_s[:, sl], b_s[:, sl], hcar[:, sl])
        hcar[:, sl] = c_new
        act[:, d + c * LANE:d + (c + 1) * LANE] = (h * _silu(rest(1, c))).astype(BF16)

    cbuf[:, 0:CONV_HALO, :] = cbuf[:, ts:ts + CONV_HALO, :]
    lbuf[:, 0:LRU_HALO, :] = lbuf[:, ts:ts + LRU_HALO, :]

    y = _dot(act[...], w_out_ref[...])
    o_ref[0] = _postnorm_residual(x_ref[0], y, mod_ref, post_g_ref)


def _even_prep(w_in, conv_w, conv_b, ln_g, ln_b, lcw, lcb, wa, ba, wx, bx, lam, w_out):
    n, d = w_in.shape[0], D_MODEL
    ng, gw = N_GATE_GROUP, GATE_GROUP
    row = lambda v: v.reshape(n, 1, d)
    cw = jnp.pad(conv_w, ((0, 0), (0, CONV_TAPS_PAD - CONV_KERNEL), (0, 0)))
    cw = cw.reshape(n, CONV_TAPS_PAD, N_SLAB, LANE).transpose(0, 2, 1, 3)
    cb = conv_b.reshape(n, N_SLAB, 1, LANE)
    hpg = gw // LRU_HEAD_DIM
    eye = jnp.eye(hpg, dtype=F32)

    def blockdiag(w):
        wg = w.reshape(n, ng, hpg, LRU_HEAD_DIM, LRU_HEAD_DIM)
        return jnp.einsum('lghij,hk->lghikj', wg, eye).reshape(n, ng, gw, gw)

    wg = jnp.concatenate([blockdiag(wa), blockdiag(wx)], axis=-1).astype(BF16)
    w_vg = jnp.concatenate([w_in[:, :, 0:2 * d], w_in[:, :, 3 * d:4 * d]], axis=2).astype(BF16)
    grouped = lambda lo: w_in[:, :, lo * d:(lo + 1) * d].reshape(n, d, ng, gw)
    w_rest = jnp.concatenate([grouped(2), grouped(4)], axis=3)
    w_rest = w_rest.transpose(0, 2, 1, 3).astype(BF16)
    return (w_vg, w_rest, cw, cb, row(ln_g), row(ln_b), lcw, row(lcb), wg,
            row(ba), row(bx), row(lam), w_out.astype(BF16))


def _even_call(x, mod, pre_g, post_g, layer, params):
    bsz, seq, d = x.shape
    ts = EVEN_TILE
    ng, gw = N_GATE_GROUP, GATE_GROUP
    j = layer // 2
    tile = pl.BlockSpec((1, ts, d), lambda b, s: (b, s, 0))
    scratch = [
        pltpu.VMEM((ts, d), BF16),
        pltpu.VMEM((ts, d), F32),
        pltpu.VMEM((ts, d), F32),
        pltpu.VMEM((ts, d), F32),
        pltpu.VMEM((ng, ts, 2 * gw), F32),
        pltpu.VMEM((N_SLAB, CONV_HALO + ts, LANE), F32),
        pltpu.VMEM((N_SLAB, ts, LANE), F32),
        pltpu.VMEM((N_SLAB, LRU_HALO + ts, LANE), F32),
        pltpu.VMEM((ts, d), F32),
        pltpu.VMEM((ts, d), BF16),
        pltpu.VMEM((ts, d), F32),
        pltpu.VMEM((ts, d), F32),
        pltpu.VMEM((SUBLANE, d), F32),
        pltpu.VMEM((ts, 2 * d), BF16),
    ]
    return pl.pallas_call(
        _even_kernel,
        grid=(bsz, seq // ts),
        in_specs=[tile, _mod_spec(mod, layer), _layer_spec(pre_g, layer), _layer_spec(post_g, layer)]
        + [_layer_spec(p, j) for p in params],
        out_specs=tile,
        out_shape=jax.ShapeDtypeStruct(x.shape, F32),
        scratch_shapes=scratch,
        compiler_params=_params(2),
        name="even_layer",
    )(x, mod, pre_g, post_g, *params)


def _mla_proj_kernel(x_ref, mod_ref, pre_g_ref, w_c_ref, w_z_ref, qn_ref, kvn_ref,
                     wuq_t_ref, wk_ref, wv_t_ref, cn_ref, sn_ref, ct_ref, st_ref,
                     qt_ref, k_ref, vt_ref, sz_ref, hbf):
    ts = x_ref.shape[1]
    half = QK_ROPE // 2
    hbf[...] = _prenorm(x_ref[0], mod_ref, pre_g_ref).astype(BF16)

    uc = _dot(hbf[...], w_c_ref[...])
    sz_ref[0] = _silu(_dot(hbf[...], w_z_ref[...])).astype(BF16)

    def rms(v, g_ref):
        ms = jnp.mean(v * v, axis=-1, keepdims=True)
        return (v * lax.rsqrt(ms + EPS) * g_ref[...]).astype(BF16)

    cqn = rms(uc[:, 0:Q_LORA], qn_ref)
    ckvn = rms(uc[:, Q_LORA:Q_LORA + KV_LORA], kvn_ref)
    kr = uc[:, Q_LORA + KV_LORA:Q_LORA + KV_LORA + QK_ROPE]

    q_t = _dot_nt(wuq_t_ref[...], cqn)
    c_t = ct_ref[0]
    s_t = st_ref[0]
    scale = (QK_NOPE + QK_ROPE) ** -0.5 * LOG2_E
    for hd in range(MLA_HEADS):
        base = hd * QK_PAD
        x1 = q_t[base + QK_NOPE:base + QK_NOPE + half]
        x2 = q_t[base + QK_NOPE + half:base + QK_NOPE + QK_ROPE]
        blk = jnp.concatenate([
            q_t[base:base + QK_NOPE],
            x1 * c_t - x2 * s_t,
            x1 * s_t + x2 * c_t,
            q_t[base + QK_NOPE + QK_ROPE:base + QK_PAD],
        ], axis=0)
        blk = (blk * scale).astype(BF16)
        for slot in range(ts // SEQ_TILE):
            qt_ref[0, hd, 0, slot] = blk[:, slot * SEQ_TILE:(slot + 1) * SEQ_TILE]

    kv_k = _dot(ckvn, wk_ref[...])
    c_n = cn_ref[0]
    s_n = sn_ref[0]
    k1 = kr[:, 0:half]
    k2 = kr[:, half:QK_ROPE]
    kr128 = jnp.concatenate([
        jnp.zeros((ts, QK_NOPE), F32),
        k1 * c_n - k2 * s_n,
        k1 * s_n + k2 * c_n,
        jnp.zeros((ts, QK_PAD - QK_NOPE - QK_ROPE), F32),
    ], axis=-1)
    for hd in range(MLA_HEADS):
        k_ref[0, hd] = (kv_k[:, hd * QK_PAD:(hd + 1) * QK_PAD] + kr128).astype(BF16)

    v_t = _dot_nt(wv_t_ref[...], ckvn)
    for hd in range(MLA_HEADS):
        vt_ref[0, hd, 0:V_HEAD, :] = v_t[hd * V_HEAD:(hd + 1) * V_HEAD].astype(BF16)
        row = lax.broadcasted_iota(jnp.int32, (V_AUG - V_HEAD, ts), 0)
        vt_ref[0, hd, V_HEAD:V_AUG, :] = jnp.where(row == 0, 1.0, 0.0).astype(BF16)


def _mla_prep(w_in, q_norm, kv_norm, w_uq, w_ukv, w_out):
    n = w_in.shape[0]
    n_c = Q_LORA + KV_LORA + QK_ROPE
    w_c = jnp.pad(w_in[:, :, :n_c], ((0, 0), (0, 0), (0, (-n_c) % LANE))).astype(BF16)
    w_z = w_in[:, :, n_c:].astype(BF16)
    wq = w_uq.reshape(n, Q_LORA, MLA_HEADS, QK_NOPE + QK_ROPE)
    wq = jnp.pad(wq, ((0, 0), (0, 0), (0, 0), (0, QK_PAD - QK_NOPE - QK_ROPE)))
    wuq_t = wq.reshape(n, Q_LORA, MLA_HEADS * QK_PAD).transpose(0, 2, 1).astype(BF16)
    wkv = w_ukv.reshape(n, KV_LORA, MLA_HEADS, QK_NOPE + V_HEAD)
    wk = jnp.pad(wkv[..., :QK_NOPE], ((0, 0), (0, 0), (0, 0), (0, QK_PAD - QK_NOPE)))
    wk = wk.reshape(n, KV_LORA, MLA_HEADS * QK_PAD).astype(BF16)
    wv_t = wkv[..., QK_NOPE:].reshape(n, KV_LORA, MLA_HEADS * V_HEAD).transpose(0, 2, 1).astype(BF16)
    proj = (w_c, w_z, q_norm.reshape(n, 1, Q_LORA), kv_norm.reshape(n, 1, KV_LORA), wuq_t, wk, wv_t)
    return proj, w_out.astype(BF16)


def _mla_proj_call(x, mod, pre_g, layer, params, cn, sn, ct, st):
    bsz, seq, d = x.shape
    ts = MLA_TILE
    nt = seq // ts
    half = QK_ROPE // 2
    j = layer // 2
    tile = pl.BlockSpec((1, ts, d), lambda b, s: (b, s, 0))
    return pl.pallas_call(
        _mla_proj_kernel,
        grid=(bsz, nt),
        in_specs=[tile, _mod_spec(mod, layer), _layer_spec(pre_g, layer)]
        + [_layer_spec(p, j) for p in params] + [
            pl.BlockSpec((1, ts, half), lambda b, s: (b, s, 0)),
            pl.BlockSpec((1, ts, half), lambda b, s: (b, s, 0)),
            pl.BlockSpec((1, half, ts), lambda b, s: (b, 0, s)),
            pl.BlockSpec((1, half, ts), lambda b, s: (b, 0, s)),
        ],
        out_specs=[
            pl.BlockSpec((1, MLA_HEADS, 1, ts // SEQ_TILE, QK_PAD, SEQ_TILE),
                         lambda b, s: (b, 0, s, 0, 0, 0)),
            pl.BlockSpec((1, MLA_HEADS, ts, QK_PAD), lambda b, s: (b, 0, s, 0)),
            pl.BlockSpec((1, MLA_HEADS, V_AUG, ts), lambda b, s: (b, 0, 0, s)),
            tile,
        ],
        out_shape=[
            jax.ShapeDtypeStruct((bsz, MLA_HEADS, nt, ts // SEQ_TILE, QK_PAD, SEQ_TILE), BF16),
            jax.ShapeDtypeStruct((bsz, MLA_HEADS, seq, QK_PAD), BF16),
            jax.ShapeDtypeStruct((bsz, MLA_HEADS, V_AUG, seq), BF16),
            jax.ShapeDtypeStruct((bsz, seq, d), BF16),
        ],
        scratch_shapes=[pltpu.VMEM((ts, d), BF16)],
        compiler_params=_params(2),
        name="mla_proj",
    )(x, mod, pre_g, *params, cn, sn, ct, st)


def _mla_attn_kernel(qt_ref, k_ref, vt_ref, sz_ref, o_ref,
                     s_lo0, s_hi0, s_lo1, s_hi1, m_lo0, m_hi0, m_lo1, m_hi1, p_lo, p_hi):
    tq = qt_ref.shape[-1]
    nt = k_ref.shape[2] // tq
    n_pair = nt // 2
    u_id = pl.program_id(2)
    heads = range(HEAD_GROUP)
    s_bufs = (((s_lo0, m_lo0), (s_hi0, m_hi0)), ((s_lo1, m_lo1), (s_hi1, m_hi1)))
    p_bufs = (p_lo, p_hi)

    def streams(slot, v_a, v_b, bufs_a, bufs_b):
        n_a = v_a + 1 if v_a is not None else 0
        n_b = v_b + 1 if v_b is not None else 0
        s_a, m_a = bufs_a
        s_b, m_bref = bufs_b
        p_scr = p_bufs[slot]
        key = lax.broadcasted_iota(jnp.int32, (tq, tq), 0)
        qry = lax.broadcasted_iota(jnp.int32, (tq, tq), 1)
        q_t = [qt_ref[0, hh, 0, slot] for hh in heads] if n_a else None
        m_b = [m_bref[hh] for hh in heads] if n_b else None
        m_part = [None] * HEAD_GROUP
        for j in range(max(n_a, n_b)):
            blk = slice(j * tq, (j + 1) * tq)
            if j < n_a:
                for hh in heads:
                    s = _dot(k_ref[0, hh, blk, :], q_t[hh])
                    if j == v_a:
                        s = jnp.where(key <= qry, s, NEG_BIG)
                    s_a[hh, blk, :] = s
                    pm = jnp.max(s.reshape(tq // MAX_PART, MAX_PART, tq), axis=0)
                    m_part[hh] = pm if m_part[hh] is None else jnp.maximum(m_part[hh], pm)
            if j < n_b:
                for hh in heads:
                    p = jnp.exp2(s_b[hh, blk, :] - m_b[hh]).astype(BF16)
                    p_scr[hh, blk, :] = p
            if n_b and j == n_b - 1:
                kend = n_b * tq
                outs = []
                for hh in heads:
                    o_aug = _dot(vt_ref[0, hh, :, 0:kend], p_scr[hh, 0:kend, :])
                    outs.append(o_aug[0:V_HEAD] / o_aug[V_HEAD:V_HEAD + 1])
                o = jnp.concatenate(outs, axis=0).T
                rows = slice(slot * tq, (slot + 1) * tq)
                o_ref[0, rows, :] = (o * sz_ref[0, rows, :].astype(F32)).astype(BF16)
        if n_a:
            for hh in heads:
                m_a[hh] = jnp.max(m_part[hh], axis=0, keepdims=True)

    def step(u):
        for slot in range(2):
            tile = lambda w: 2 * w + slot
            v_a = tile(u) if u < n_pair else None
            v_b = tile(u - 1) if u > 0 else None
            streams(slot, v_a, v_b, s_bufs[u % 2][slot], s_bufs[(u - 1) % 2][slot])

    for u in range(n_pair + 1):
        pl.when(u_id == u)(functools.partial(step, u))


def _mla_attn_call(qt, k, vt, sz):
    bsz, heads, n_pair, _, _, ts = qt.shape
    seq = 2 * n_pair * ts
    wide = HEAD_GROUP * V_HEAD
    lo_rows, hi_rows = seq - ts, seq
    prev = lambda u: jnp.maximum(u - 1, 0)
    io_spec = pl.BlockSpec((1, 2 * ts, wide), lambda b, p, u: (b, prev(u), p))
    s_shapes = [pltpu.VMEM((HEAD_GROUP, rows, ts), F32) for _ in range(2) for rows in (lo_rows, hi_rows)]
    m_shapes = [pltpu.VMEM((HEAD_GROUP, 1, ts), F32) for _ in range(4)]
    p_shapes = [pltpu.VMEM((HEAD_GROUP, rows, ts), BF16) for rows in (lo_rows, hi_rows)]
    return pl.pallas_call(
        _mla_attn_kernel,
        grid=(bsz, heads // HEAD_GROUP, n_pair + 1),
        in_specs=[
            pl.BlockSpec((1, HEAD_GROUP, 1, 2, QK_PAD, ts),
                         lambda b, p, u: (b, p, jnp.minimum(u, n_pair - 1), 0, 0, 0)),
            pl.BlockSpec((1, HEAD_GROUP, seq, QK_PAD), lambda b, p, u: (b, p, 0, 0)),
            pl.BlockSpec((1, HEAD_GROUP, V_AUG, seq), lambda b, p, u: (b, p, 0, 0)),
            io_spec,
        ],
        out_specs=io_spec,
        out_shape=jax.ShapeDtypeStruct((bsz, seq, heads * V_HEAD), BF16),
        scratch_shapes=s_shapes + m_shapes + p_shapes,
        compiler_params=_params(3),
        name="mla_attn",
    )(qt, k, vt, sz)


def _mla_out_kernel(x_ref, o_ref_in, mod_ref, post_g_ref, w_out_ref, out_ref):
    y = _dot(o_ref_in[0], w_out_ref[...])
    out_ref[0] = _postnorm_residual(x_ref[0], y, mod_ref, post_g_ref)


def _mla_out_call(x, o, mod, post_g, layer, w_out):
    bsz, seq, d = x.shape
    ts = OUT_TILE
    nt = seq // ts
    tile = pl.BlockSpec((1, ts, d), lambda b, s: (b, s, 0))
    return pl.pallas_call(
        _mla_out_kernel,
        grid=(bsz, nt),
        in_specs=[
            tile,
            tile,
            _mod_spec(mod, layer),
            _layer_spec(post_g, layer),
            _layer_spec(w_out, layer // 2),
        ],
        out_specs=tile,
        out_shape=jax.ShapeDtypeStruct(x.shape, F32),
        compiler_params=_params(2),
        name="mla_out",
    )(x, o, mod, post_g, w_out)


def kernel(x, c, positions, ada_w, ada_b, pre_g, post_g, ev_w_in, ev_conv_w, ev_conv_b, ev_ln_g, ev_ln_b, ev_lru_conv_w, ev_lru_conv_b, ev_lru_wa, ev_lru_ba, ev_lru_wx, ev_lru_bx, ev_lru_lam, ev_w_out, od_w_in, od_q_norm, od_kv_norm, od_w_uq, od_w_ukv, od_w_out):
    depth = ada_w.shape[0]
    bsz, seq, d = x.shape
    assert d == D_MODEL and all(seq % t == 0 for t in (MLA_TILE, EVEN_TILE, OUT_TILE))
    mod = _ada_call(c, ada_w, ada_b).reshape(depth, bsz, 3, d)
    cn, sn, ct, st = _rope_call(positions)
    pre_g = pre_g.reshape(depth, 1, d)
    post_g = post_g.reshape(depth, 1, d)
    even_params = _even_prep(ev_w_in, ev_conv_w, ev_conv_b, ev_ln_g, ev_ln_b, ev_lru_conv_w,
                             ev_lru_conv_b, ev_lru_wa, ev_lru_ba, ev_lru_wx, ev_lru_bx, ev_lru_lam,
                             ev_w_out)
    proj_params, od_w_out_b = _mla_prep(od_w_in, od_q_norm, od_kv_norm, od_w_uq, od_w_ukv, od_w_out)
    for layer in range(depth):
        if layer % 2 == 0:
            x = _even_call(x, mod, pre_g, post_g, layer, even_params)
        else:
            qt, k, vt, sz = _mla_proj_call(x, mod, pre_g, layer, proj_params, cn, sn, ct, st)
            o = _mla_attn_call(qt, k, vt, sz)
            x = _mla_out_call(x, o, mod, post_g, layer, od_w_out_b)
    return x
```

```python
import functools

import jax
import jax.numpy as jnp
from jax import lax
from jax.experimental import pallas as pl
from jax.experimental.pallas import tpu as pltpu

F32 = jnp.float32
BF16 = jnp.bfloat16

D_MODEL = 1024
CONV_KERNEL = 31
LRU_CONV_KERNEL = 4
LRU_HEADS = 16
LRU_HEAD_DIM = D_MODEL // LRU_HEADS
LRU_C = 8.0
MLA_HEADS = 16
QK_NOPE = 64
QK_ROPE = 32
V_HEAD = 64
Q_LORA = 256
KV_LORA = 256
ROPE_THETA = 10000.0
EPS = 1e-6

LANE = 128
SUBLANE = 8
MXU_DIM = 256

SEQ_TILE = 256
MLA_TILE = 2 * SEQ_TILE
EVEN_TILE = 512
OUT_TILE = 2048
N_SLAB = D_MODEL // LANE
CONV_HALO = 32
LRU_HALO = 8
CONV_TAPS_PAD = 32
GATE_GROUP = MXU_DIM
N_GATE_GROUP = D_MODEL // GATE_GROUP
CONV_ROWS = 64
QK_PAD = LANE
HEAD_GROUP = 4
MAX_PART = 32
V_AUG = V_HEAD + 16
LOG2_E = 1.4426950408889634
NEG_BIG = -1e30

VMEM_LIMIT_BYTES = 56 * 1024 * 1024


def _dot(a, b):
    return jnp.dot(a, b, preferred_element_type=F32)


def _dot_nt(a, b):
    return lax.dot_general(a, b, (((1,), (1,)), ((), ())), preferred_element_type=F32)


def _sigmoid(x):
    return 0.5 * jnp.tanh(0.5 * x) + 0.5


def _silu(x):
    h = 0.5 * x
    return h + h * jnp.tanh(h)


def _layer_spec(stacked, j):
    rest = stacked.shape[1:]
    return pl.BlockSpec((None,) + rest, lambda *_: (j,) + (0,) * len(rest),
                        pipeline_mode=pl.Buffered(1))


def _mod_spec(mod, layer):
    return pl.BlockSpec((None, 1) + mod.shape[2:], lambda b, *_: (layer, b, 0, 0))


def _params(n_grid):
    return pltpu.CompilerParams(
        dimension_semantics=("arbitrary",) * n_grid, vmem_limit_bytes=VMEM_LIMIT_BYTES)


def _ada_kernel(c_ref, w_ref, b_ref, o_ref):
    c = c_ref[...]
    o_ref[0] = jnp.dot(_silu(c), w_ref[0], preferred_element_type=F32,
                       precision=lax.Precision.HIGHEST) + b_ref[0]


def _ada_call(c, ada_w, ada_b):
    depth, d, d3 = ada_w.shape
    bsz = c.shape[0]
    n_col = d3 // d
    return pl.pallas_call(
        _ada_kernel,
        grid=(depth, n_col),
        in_specs=[
            pl.BlockSpec((bsz, d), lambda l, j: (0, 0)),
            pl.BlockSpec((1, d, d), lambda l, j: (l, 0, j)),
            pl.BlockSpec((1, 1, d), lambda l, j: (l, 0, j)),
        ],
        out_specs=pl.BlockSpec((1, bsz, d), lambda l, j: (l, 0, j)),
        out_shape=jax.ShapeDtypeStruct((depth, bsz, d3), F32),
        compiler_params=_params(2),
        name="ada_mod",
    )(c, ada_w, ada_b.reshape(depth, 1, d3))


def _rope_kernel(posrep_ref, invrep_ref, pos_ref, invcol_ref, cn_ref, sn_ref, ct_ref, st_ref):
    ang_n = posrep_ref[0].astype(F32) * invrep_ref[...]
    cn_ref[0] = jnp.cos(ang_n)
    sn_ref[0] = jnp.sin(ang_n)
    ang_t = invcol_ref[...] * pos_ref[0].astype(F32)
    ct_ref[0] = jnp.cos(ang_t)
    st_ref[0] = jnp.sin(ang_t)


def _rope_call(positions):
    bsz, seq = positions.shape
    half = QK_ROPE // 2
    rep = LANE // half
    inv = ROPE_THETA ** (-jnp.arange(0, QK_ROPE, 2, dtype=F32) / QK_ROPE)
    posrep = jnp.repeat(positions, half, axis=1).reshape(bsz, seq // rep, LANE)
    invrep = jnp.tile(inv, rep).reshape(1, LANE)
    n_spec = pl.BlockSpec((1, seq // rep, LANE), lambda b: (b, 0, 0))
    t_spec = pl.BlockSpec((1, half, seq), lambda b: (b, 0, 0))
    cn, sn, ct, st = pl.pallas_call(
        _rope_kernel,
        grid=(bsz,),
        in_specs=[
            n_spec,
            pl.BlockSpec((1, LANE), lambda b: (0, 0)),
            pl.BlockSpec((1, 1, seq), lambda b: (b, 0, 0)),
            pl.BlockSpec((half, 1), lambda b: (0, 0)),
        ],
        out_specs=[n_spec, n_spec, t_spec, t_spec],
        out_shape=[jax.ShapeDtypeStruct((bsz, seq // rep, LANE), F32)] * 2
        + [jax.ShapeDtypeStruct((bsz, half, seq), F32)] * 2,
        compiler_params=_params(1),
        name="rope_tables",
    )(posrep, invrep, positions.reshape(bsz, 1, seq), inv.reshape(half, 1))
    return cn.reshape(bsz, seq, half), sn.reshape(bsz, seq, half), ct, st


def _prenorm(x, mod_ref, pre_g_ref):
    ms = jnp.mean(x * x, axis=-1, keepdims=True)
    shift = mod_ref[0, 0:1, :]
    scale = mod_ref[0, 1:2, :]
    gmul = pre_g_ref[...] * (1.0 + scale)
    return x * lax.rsqrt(ms + EPS) * gmul + shift


def _postnorm_residual(x, y, mod_ref, post_g_ref):
    ms = jnp.mean(y * y, axis=-1, keepdims=True)
    gate = mod_ref[0, 2:3, :]
    return x + (gate * post_g_ref[...]) * (y * lax.rsqrt(ms + EPS))


def _scan_slab(a, b, carry):
    t = a.shape[0]
    groups = t // SUBLANE
    a = a.reshape(groups, SUBLANE, LANE)
    b = b.reshape(groups, SUBLANE, LANE)
    row = lax.broadcasted_iota(jnp.int32, a.shape, 1)
    d = 1
    while d < SUBLANE:
        ok = row >= d
        b = jnp.where(ok, a, 0.0) * pltpu.roll(b, d, axis=1) + b
        a = a * jnp.where(ok, pltpu.roll(a, d, axis=1), 1.0)
        d *= 2
    outs = []
    c = carry
    for v in range(groups):
        hv = b[v] + a[v] * c
        outs.append(hv)
        c = jnp.broadcast_to(hv[SUBLANE - 1:SUBLANE, :], (SUBLANE, LANE))
    return jnp.concatenate(outs, axis=0), c


def _even_kernel(x_ref, mod_ref, pre_g_ref, post_g_ref, w_vg_ref, w_rest_ref, cw_ref, cb_ref, lng_ref,
                 lnb_ref, lcw_ref, lcb_ref, wg_ref, ba_ref, bx_ref, lam_ref, w_out_ref, o_ref,
                 hbf, u0, u1, u2, ur, cbuf, cv, lbuf, xcf, xcb, a_s, b_s, hcar, act):
    ts = x_ref.shape[1]
    d = D_MODEL
    gw = GATE_GROUP
    spg = gw // LANE

    def rest(kind, c):
        lo = kind * gw + (c % spg) * LANE
        return ur[c // spg, :, lo:lo + LANE]

    @pl.when(pl.program_id(1) == 0)
    def _():
        cbuf[:, 0:CONV_HALO, :] = jnp.zeros((N_SLAB, CONV_HALO, LANE), F32)
        lbuf[:, 0:LRU_HALO, :] = jnp.zeros((N_SLAB, LRU_HALO, LANE), F32)
        hcar[...] = jnp.zeros(hcar.shape, F32)

    hbf[...] = _prenorm(x_ref[0], mod_ref, pre_g_ref).astype(BF16)

    u0[...] = _dot(hbf[...], w_vg_ref[:, 0:d])
    u1[...] = _dot(hbf[...], w_vg_ref[:, d:2 * d])
    u2[...] = _dot(hbf[...], w_vg_ref[:, 2 * d:3 * d])
    for g in range(N_GATE_GROUP):
        ur[g] = _dot(hbf[...], w_rest_ref[g])
    for c in range(N_SLAB):
        sl = slice(c * LANE, (c + 1) * LANE)
        cbuf[c, CONV_HALO:CONV_HALO + ts, :] = u0[:, sl] * _sigmoid(u1[:, sl])

    first = CONV_HALO - (CONV_KERNEL - 1)
    tap_groups = [[j for j in range(CONV_KERNEL) if (first + j) % SUBLANE == r] for r in range(SUBLANE)]

    def conv_slab(c, carry):
        w = cw_ref[c]
        for r0 in range(0, ts, CONV_ROWS):
            acc = jnp.broadcast_to(cb_ref[c], (CONV_ROWS, LANE))
            for taps in tap_groups:
                start = first + taps[0] + r0
                win = cbuf[c, start:start + CONV_ROWS + SUBLANE * (len(taps) - 1), :]
                for q, j in enumerate(taps):
                    acc = acc + win[SUBLANE * q:SUBLANE * q + CONV_ROWS] * w[j:j + 1, :]
            cv[c, r0:r0 + CONV_ROWS, :] = acc
        return carry

    lax.fori_loop(0, N_SLAB, conv_slab, 0)

    tot = cv[0]
    for c in range(1, N_SLAB):
        tot = tot + cv[c]
    mu = jnp.broadcast_to(jnp.sum(tot, axis=-1, keepdims=True) * (1.0 / d), (ts, LANE))
    sq = jnp.zeros((ts, LANE), F32)
    for c in range(N_SLAB):
        dc = cv[c] - mu
        sq = sq + dc * dc
    rstd = lax.rsqrt(jnp.sum(sq, axis=-1, keepdims=True) * (1.0 / d) + EPS)
    rstd = jnp.broadcast_to(rstd, (ts, LANE))
    for c in range(N_SLAB):
        sl = slice(c * LANE, (c + 1) * LANE)
        yn = (cv[c] - mu) * rstd * lng_ref[:, sl] + lnb_ref[:, sl]
        act[:, sl] = (_silu(yn) * _silu(rest(0, c))).astype(BF16)

    for c in range(N_SLAB):
        sl = slice(c * LANE, (c + 1) * LANE)
        lbuf[c, LRU_HALO:LRU_HALO + ts, :] = u2[:, sl]
        acc = jnp.broadcast_to(lcb_ref[:, sl], (ts, LANE))
        for j in range(LRU_CONV_KERNEL):
            start = LRU_HALO - (LRU_CONV_KERNEL - 1) + j
            acc = acc + lbuf[c, start:start + ts, :] * lcw_ref[j:j + 1, sl]
        xcf[:, sl] = acc
        xcb[:, sl] = acc.astype(BF16)

    nl = -lam_ref[...]
    softplus = jnp.maximum(nl, 0.0) + jnp.log1p(jnp.exp(-jnp.abs(nl)))
    for g in range(N_GATE_GROUP):
        sl = slice(g * gw, (g + 1) * gw)
        rg = _dot(xcb[:, sl], wg_ref[g])
        r = _sigmoid(rg[:, 0:gw] + ba_ref[:, sl])
        i = _sigmoid(rg[:, gw:2 * gw] + bx_ref[:, sl])
        a = jnp.exp((-LRU_C) * r * softplus[:, sl])
        a_s[:, sl] = a
        b_s[:, sl] = jnp.sqrt(1.0 - a * a) * i * xcf[:, sl]

    for c in range(N_SLAB):
        sl = slice(c * LANE, (c + 1) * LANE)
        h, c_new = _scan_slab(a_s[:, sl], b_s[:, sl], hcar[:, sl])
        hcar[:, sl] = c_new
        act[:, d + c * LANE:d + (c + 1) * LANE] = (h * _silu(rest(1, c))).astype(BF16)

    cbuf[:, 0:CONV_HALO, :] = cbuf[:, ts:ts + CONV_HALO, :]
    lbuf[:, 0:LRU_HALO, :] = lbuf[:, ts:ts + LRU_HALO, :]

    y = _dot(act[...], w_out_ref[...])
    o_ref[0] = _postnorm_residual(x_ref[0], y, mod_ref, post_g_ref)


def _even_prep(w_in, conv_w, conv_b, ln_g, ln_b, lcw, lcb, wa, ba, wx, bx, lam, w_out):
    n, d = w_in.shape[0], D_MODEL
    ng, gw = N_GATE_GROUP, GATE_GROUP
    row = lambda v: v.reshape(n, 1, d)
    cw = jnp.pad(conv_w, ((0, 0), (0, CONV_TAPS_PAD - CONV_KERNEL), (0, 0)))
    cw = cw.reshape(n, CONV_TAPS_PAD, N_SLAB, LANE).transpose(0, 2, 1, 3)
    cb = conv_b.reshape(n, N_SLAB, 1, LANE)
    hpg = gw // LRU_HEAD_DIM
    eye = jnp.eye(hpg, dtype=F32)

    def blockdiag(w):
        wg = w.reshape(n, ng, hpg, LRU_HEAD_DIM, LRU_HEAD_DIM)
        return jnp.einsum('lghij,hk->lghikj', wg, eye).reshape(n, ng, gw, gw)

    wg = jnp.concatenate([blockdiag(wa), blockdiag(wx)], axis=-1).astype(BF16)
    w_vg = jnp.concatenate([w_in[:, :, 0:2 * d], w_in[:, :, 3 * d:4 * d]], axis=2).astype(BF16)
    grouped = lambda lo: w_in[:, :, lo * d:(lo + 1) * d].reshape(n, d, ng, gw)
    w_rest = jnp.concatenate([grouped(2), grouped(4)], axis=3)
    w_rest = w_rest.transpose(0, 2, 1, 3).astype(BF16)
    return (w_vg, w_rest, cw, cb, row(ln_g), row(ln_b), lcw, row(lcb), wg,
            row(ba), row(bx), row(lam), w_out.astype(BF16))


def _even_call(x, mod, pre_g, post_g, layer, params):
    bsz, seq, d = x.shape
    ts = EVEN_TILE
    ng, gw = N_GATE_GROUP, GATE_GROUP
    j = layer // 2
    tile = pl.BlockSpec((1, ts, d), lambda b, s: (b, s, 0))
    scratch = [
        pltpu.VMEM((ts, d), BF16),
        pltpu.VMEM((ts, d), F32),
        pltpu.VMEM((ts, d), F32),
        pltpu.VMEM((ts, d), F32),
        pltpu.VMEM((ng, ts, 2 * gw), F32),
        pltpu.VMEM((N_SLAB, CONV_HALO + ts, LANE), F32),
        pltpu.VMEM((N_SLAB, ts, LANE), F32),
        pltpu.VMEM((N_SLAB, LRU_HALO + ts, LANE), F32),
        pltpu.VMEM((ts, d), F32),
        pltpu.VMEM((ts, d), BF16),
        pltpu.VMEM((ts, d), F32),
        pltpu.VMEM((ts, d), F32),
        pltpu.VMEM((SUBLANE, d), F32),
        pltpu.VMEM((ts, 2 * d), BF16),
    ]
    return pl.pallas_call(
        _even_kernel,
        grid=(bsz, seq // ts),
        in_specs=[tile, _mod_spec(mod, layer), _layer_spec(pre_g, layer), _layer_spec(post_g, layer)]
        + [_layer_spec(p, j) for p in params],
        out_specs=tile,
        out_shape=jax.ShapeDtypeStruct(x.shape, F32),
        scratch_shapes=scratch,
        compiler_params=_params(2),
        name="even_layer",
    )(x, mod, pre_g, post_g, *params)


def _mla_proj_kernel(x_ref, mod_ref, pre_g_ref, w_c_ref, w_z_ref, qn_ref, kvn_ref,
                     wuq_t_ref, wk_ref, wv_t_ref, cn_ref, sn_ref, ct_ref, st_ref,
                     qt_ref, k_ref, vt_ref, sz_ref, hbf):
    ts = x_ref.shape[1]
    half = QK_ROPE // 2
    hbf[...] = _prenorm(x_ref[0], mod_ref, pre_g_ref).astype(BF16)

    uc = _dot(hbf[...], w_c_ref[...])
    sz_ref[0] = _silu(_dot(hbf[...], w_z_ref[...])).astype(BF16)

    def rms(v, g_ref):
        ms = jnp.mean(v * v, axis=-1, keepdims=True)
        return (v * lax.rsqrt(ms + EPS) * g_ref[...]).astype(BF16)

    cqn = rms(uc[:, 0:Q_LORA], qn_ref)
    ckvn = rms(uc[:, Q_LORA:Q_LORA + KV_LORA], kvn_ref)
    kr = uc[:, Q_LORA + KV_LORA:Q_LORA + KV_LORA + QK_ROPE]

    q_t = _dot_nt(wuq_t_ref[...], cqn)
    c_t = ct_ref[0]
    s_t = st_ref[0]
    scale = (QK_NOPE + QK_ROPE) ** -0.5 * LOG2_E
    for hd in range(MLA_HEADS):
        base = hd * QK_PAD
        x1 = q_t[base + QK_NOPE:base + QK_NOPE + half]
        x2 = q_t[base + QK_NOPE + half:base + QK_NOPE + QK_ROPE]
        blk = jnp.concatenate([
            q_t[base:base + QK_NOPE],
            x1 * c_t - x2 * s_t,
            x1 * s_t + x2 * c_t,
            q_t[base + QK_NOPE + QK_ROPE:base + QK_PAD],
        ], axis=0)
        blk = (blk * scale).astype(BF16)
        for slot in range(ts // SEQ_TILE):
            qt_ref[0, hd, 0, slot] = blk[:, slot * SEQ_TILE:(slot + 1) * SEQ_TILE]

    kv_k = _dot(ckvn, wk_ref[...])
    c_n = cn_ref[0]
    s_n = sn_ref[0]
    k1 = kr[:, 0:half]
    k2 = kr[:, half:QK_ROPE]
    kr128 = jnp.concatenate([
        jnp.zeros((ts, QK_NOPE), F32),
        k1 * c_n - k2 * s_n,
        k1 * s_n + k2 * c_n,
        jnp.zeros((ts, QK_PAD - QK_NOPE - QK_ROPE), F32),
    ], axis=-1)
    for hd in range(MLA_HEADS):
        k_ref[0, hd] = (kv_k[:, hd * QK_PAD:(hd + 1) * QK_PAD] + kr128).astype(BF16)

    v_t = _dot_nt(wv_t_ref[...], ckvn)
    for hd in range(MLA_HEADS):
        vt_ref[0, hd, 0:V_HEAD, :] = v_t[hd * V_HEAD:(hd + 1) * V_HEAD].astype(BF16)
        row = lax.broadcasted_iota(jnp.int32, (V_AUG - V_HEAD, ts), 0)
        vt_ref[0, hd, V_HEAD:V_AUG, :] = jnp.where(row == 0, 1.0, 0.0).astype(BF16)


def _mla_prep(w_in, q_norm, kv_norm, w_uq, w_ukv, w_out):
    n = w_in.shape[0]
    n_c = Q_LORA + KV_LORA + QK_ROPE
    w_c = jnp.pad(w_in[:, :, :n_c], ((0, 0), (0, 0), (0, (-n_c) % LANE))).astype(BF16)
    w_z = w_in[:, :, n_c:].astype(BF16)
    wq = w_uq.reshape(n, Q_LORA, MLA_HEADS, QK_NOPE + QK_ROPE)
    wq = jnp.pad(wq, ((0, 0), (0, 0), (0, 0), (0, QK_PAD - QK_NOPE - QK_ROPE)))
    wuq_t = wq.reshape(n, Q_LORA, MLA_HEADS * QK_PAD).transpose(0, 2, 1).astype(BF16)
    wkv = w_ukv.reshape(n, KV_LORA, MLA_HEADS, QK_NOPE + V_HEAD)
    wk = jnp.pad(wkv[..., :QK_NOPE], ((0, 0), (0, 0), (0, 0), (0, QK_PAD - QK_NOPE)))
    wk = wk.reshape(n, KV_LORA, MLA_HEADS * QK_PAD).astype(BF16)
    wv_t = wkv[..., QK_NOPE:].reshape(n, KV_LORA, MLA_HEADS * V_HEAD).transpose(0, 2, 1).astype(BF16)
    proj = (w_c, w_z, q_norm.reshape(n, 1, Q_LORA), kv_norm.reshape(n, 1, KV_LORA), wuq_t, wk, wv_t)
    return proj, w_out.astype(BF16)


def _mla_proj_call(x, mod, pre_g, layer, params, cn, sn, ct, st):
    bsz, seq, d = x.shape
    ts = MLA_TILE
    nt = seq // ts
    half = QK_ROPE // 2
    j = layer // 2
    tile = pl.BlockSpec((1, ts, d), lambda b, s: (b, s, 0))
    return pl.pallas_call(
        _mla_proj_kernel,
        grid=(bsz, nt),
        in_specs=[tile, _mod_spec(mod, layer), _layer_spec(pre_g, layer)]
        + [_layer_spec(p, j) for p in params] + [
            pl.BlockSpec((1, ts, half), lambda b, s: (b, s, 0)),
            pl.BlockSpec((1, ts, half), lambda b, s: (b, s, 0)),
            pl.BlockSpec((1, half, ts), lambda b, s: (b, 0, s)),
            pl.BlockSpec((1, half, ts), lambda b, s: (b, 0, s)),
        ],
        out_specs=[
            pl.BlockSpec((1, MLA_HEADS, 1, ts // SEQ_TILE, QK_PAD, SEQ_TILE),
                         lambda b, s: (b, 0, s, 0, 0, 0)),
            pl.BlockSpec((1, MLA_HEADS, ts, QK_PAD), lambda b, s: (b, 0, s, 0)),
            pl.BlockSpec((1, MLA_HEADS, V_AUG, ts), lambda b, s: (b, 0, 0, s)),
            tile,
        ],
        out_shape=[
            jax.ShapeDtypeStruct((bsz, MLA_HEADS, nt, ts // SEQ_TILE, QK_PAD, SEQ_TILE), BF16),
            jax.ShapeDtypeStruct((bsz, MLA_HEADS, seq, QK_PAD), BF16),
            jax.ShapeDtypeStruct((bsz, MLA_HEADS, V_AUG, seq), BF16),
            jax.ShapeDtypeStruct((bsz, seq, d), BF16),
        ],
        scratch_shapes=[pltpu.VMEM((ts, d), BF16)],
        compiler_params=_params(2),
        name="mla_proj",
    )(x, mod, pre_g, *params, cn, sn, ct, st)


def _mla_attn_kernel(qt_ref, k_ref, vt_ref, sz_ref, o_ref,
                     s_lo0, s_hi0, s_lo1, s_hi1, m_lo0, m_hi0, m_lo1, m_hi1, p_lo, p_hi):
    tq = qt_ref.shape[-1]
    nt = k_ref.shape[2] // tq
    n_pair = nt // 2
    u_id = pl.program_id(2)
    heads = range(HEAD_GROUP)
    s_bufs = (((s_lo0, m_lo0), (s_hi0, m_hi0)), ((s_lo1, m_lo1), (s_hi1, m_hi1)))
    p_bufs = (p_lo, p_hi)

    def streams(slot, v_a, v_b, bufs_a, bufs_b):
        n_a = v_a + 1 if v_a is not None else 0
        n_b = v_b + 1 if v_b is not None else 0
        s_a, m_a = bufs_a
        s_b, m_bref = bufs_b
        p_scr = p_bufs[slot]
        key = lax.broadcasted_iota(jnp.int32, (tq, tq), 0)
        qry = lax.broadcasted_iota(jnp.int32, (tq, tq), 1)
        q_t = [qt_ref[0, hh, 0, slot] for hh in heads] if n_a else None
        m_b = [m_bref[hh] for hh in heads] if n_b else None
        m_part = [None] * HEAD_GROUP
        for j in range(max(n_a, n_b)):
            blk = slice(j * tq, (j + 1) * tq)
            if j < n_a:
                for hh in heads:
                    s = _dot(k_ref[0, hh, blk, :], q_t[hh])
                    if j == v_a:
                        s = jnp.where(key <= qry, s, NEG_BIG)
                    s_a[hh, blk, :] = s
                    pm = jnp.max(s.reshape(tq // MAX_PART, MAX_PART, tq), axis=0)
                    m_part[hh] = pm if m_part[hh] is None else jnp.maximum(m_part[hh], pm)
            if j < n_b:
                for hh in heads:
                    p = jnp.exp2(s_b[hh, blk, :] - m_b[hh]).astype(BF16)
                    p_scr[hh, blk, :] = p
            if n_b and j == n_b - 1:
                kend = n_b * tq
                outs = []
                for hh in heads:
                    o_aug = _dot(vt_ref[0, hh, :, 0:kend], p_scr[hh, 0:kend, :])
                    outs.append(o_aug[0:V_HEAD] / o_aug[V_HEAD:V_HEAD + 1])
                o = jnp.concatenate(outs, axis=0).T
                rows = slice(slot * tq, (slot + 1) * tq)
                o_ref[0, rows, :] = (o * sz_ref[0, rows, :].astype(F32)).astype(BF16)
        if n_a:
            for hh in heads:
                m_a[hh] = jnp.max(m_part[hh], axis=0, keepdims=True)

    def step(u):
        for slot in range(2):
            tile = lambda w: 2 * w + slot
            v_a = tile(u) if u < n_pair else None
            v_b = tile(u - 1) if u > 0 else None
            streams(slot, v_a, v_b, s_bufs[u % 2][slot], s_bufs[(u - 1) % 2][slot])

    for u in range(n_pair + 1):
        pl.when(u_id == u)(functools.partial(step, u))


def _mla_attn_call(qt, k, vt, sz):
    bsz, heads, n_pair, _, _, ts = qt.shape
    seq = 2 * n_pair * ts
    wide = HEAD_GROUP * V_HEAD
    lo_rows, hi_rows = seq - ts, seq
    prev = lambda u: jnp.maximum(u - 1, 0)
    io_spec = pl.BlockSpec((1, 2 * ts, wide), lambda b, p, u: (b, prev(u), p))
    s_shapes = [pltpu.VMEM((HEAD_GROUP, rows, ts), F32) for _ in range(2) for rows in (lo_rows, hi_rows)]
    m_shapes = [pltpu.VMEM((HEAD_GROUP, 1, ts), F32) for _ in range(4)]
    p_shapes = [pltpu.VMEM((HEAD_GROUP, rows, ts), BF16) for rows in (lo_rows, hi_rows)]
    return pl.pallas_call(
        _mla_attn_kernel,
        grid=(bsz, heads // HEAD_GROUP, n_pair + 1),
        in_specs=[
            pl.BlockSpec((1, HEAD_GROUP, 1, 2, QK_PAD, ts),
                         lambda b, p, u: (b, p, jnp.minimum(u, n_pair - 1), 0, 0, 0)),
            pl.BlockSpec((1, HEAD_GROUP, seq, QK_PAD), lambda b, p, u: (b, p, 0, 0)),
            pl.BlockSpec((1, HEAD_GROUP, V_AUG, seq), lambda b, p, u: (b, p, 0, 0)),
            io_spec,
        ],
        out_specs=io_spec,
        out_shape=jax.ShapeDtypeStruct((bsz, seq, heads * V_HEAD), BF16),
        scratch_shapes=s_shapes + m_shapes + p_shapes,
        compiler_params=_params(3),
        name="mla_attn",
    )(qt, k, vt, sz)


def _mla_out_kernel(x_ref, o_ref_in, mod_ref, post_g_ref, w_out_ref, out_ref):
    y = _dot(o_ref_in[0], w_out_ref[...])
    out_ref[0] = _postnorm_residual(x_ref[0], y, mod_ref, post_g_ref)


def _mla_out_call(x, o, mod, post_g, layer, w_out):
    bsz, seq, d = x.shape
    ts = OUT_TILE
    nt = seq // ts
    tile = pl.BlockSpec((1, ts, d), lambda b, s: (b, s, 0))
    return pl.pallas_call(
        _mla_out_kernel,
        grid=(bsz, nt),
        in_specs=[
            tile,
            tile,
            _mod_spec(mod, layer),
            _layer_spec(post_g, layer),
            _layer_spec(w_out, layer // 2),
        ],
        out_specs=tile,
        out_shape=jax.ShapeDtypeStruct(x.shape, F32),
        compiler_params=_params(2),
        name="mla_out",
    )(x, o, mod, post_g, w_out)


def kernel(x, c, positions, ada_w, ada_b, pre_g, post_g, ev_w_in, ev_conv_w, ev_conv_b, ev_ln_g, ev_ln_b, ev_lru_conv_w, ev_lru_conv_b, ev_lru_wa, ev_lru_ba, ev_lru_wx, ev_lru_bx, ev_lru_lam, ev_w_out, od_w_in, od_q_norm, od_kv_norm, od_w_uq, od_w_ukv, od_w_out):
    depth = ada_w.shape[0]
    bsz, seq, d = x.shape
    assert d == D_MODEL and all(seq % t == 0 for t in (MLA_TILE, EVEN_TILE, OUT_TILE))
    mod = _ada_call(c, ada_w, ada_b).reshape(depth, bsz, 3, d)
    cn, sn, ct, st = _rope_call(positions)
    pre_g = pre_g.reshape(depth, 1, d)
    post_g = post_g.reshape(depth, 1, d)
    even_params = _even_prep(ev_w_in, ev_conv_w, ev_conv_b, ev_ln_g, ev_ln_b, ev_lru_conv_w,
                             ev_lru_conv_b, ev_lru_wa, ev_lru_ba, ev_lru_wx, ev_lru_bx, ev_lru_lam,
                             ev_w_out)
    proj_params, od_w_out_b = _mla_prep(od_w_in, od_q_norm, od_kv_norm, od_w_uq, od_w_ukv, od_w_out)
    for layer in range(depth):
        if layer % 2 == 0:
            x = _even_call(x, mod, pre_g, post_g, layer, even_params)
        else:
            qt, k, vt, sz = _mla_proj_call(x, mod, pre_g, layer, proj_params, cn, sn, ct, st)
            o = _mla_attn_call(qt, k, vt, sz)
            x = _mla_out_call(x, o, mod, post_g, layer, od_w_out_b)
    return x
```
